```python
import math
import jax, jax.numpy as jnp
from jax import lax
import numpy as np

D_MODEL = 1024
BATCH = 8
SEQ = 8192
DEPTH = 2

N_MIXERS = 4
GROUP_WIDTH = D_MODEL // N_MIXERS
MIX_WIDTH = N_MIXERS * GROUP_WIDTH
HEAD_DIM = 64
N_HEADS = GROUP_WIDTH // HEAD_DIM
IN_WIDTH = 11 * GROUP_WIDTH
SHORT_CONV_WIDTH = 3
CONFORMER_CONV_WIDTH = 31
SB_Q_BLOCK = 128
MOBA_BLOCK = 256
MOBA_TOPK = 3
MOBA_Q_CHUNK = 64
ROPE_THETA = 10000.0
D_FF = 4 * D_MODEL
EPS = 1e-6

kernel_name = 'hybrid_parallel_groups_sb_moba_conv'


def rms_norm(x, g):
    xf = x.astype(jnp.float32)
    y = xf * lax.rsqrt(jnp.mean(xf * xf, axis=-1, keepdims=True) + EPS)
    return (y * g.astype(jnp.float32)).astype(x.dtype)


def layer_norm(x, g, b):
    xf = x.astype(jnp.float32)
    mu = jnp.mean(xf, axis=-1, keepdims=True)
    var = jnp.mean(jnp.square(xf - mu), axis=-1, keepdims=True)
    y = (xf - mu) * lax.rsqrt(var + EPS)
    return (y * g.astype(jnp.float32) + b.astype(jnp.float32)).astype(x.dtype)


def causal_depthwise_conv(x, w):
    k = w.shape[0]
    ch = x.shape[-1]
    return lax.conv_general_dilated(
        x, w.astype(x.dtype)[:, None, :], window_strides=(1,), padding=[(k - 1, 0)],
        dimension_numbers=('NWC', 'WIO', 'NWC'), feature_group_count=ch)


def to_heads(t):
    b, s, _ = t.shape
    return t.reshape(b, s, N_HEADS, HEAD_DIM).transpose(0, 2, 1, 3)


def from_heads(t):
    b, h, s, d = t.shape
    return t.transpose(0, 2, 1, 3).reshape(b, s, h * d)


def apply_rope(t, positions):
    half = HEAD_DIM // 2
    inv_freq = jnp.exp(-math.log(ROPE_THETA) * jnp.arange(half, dtype=jnp.float32) / half)
    ang = positions.astype(jnp.float32)[:, None, :, None] * inv_freq
    cos, sin = jnp.cos(ang), jnp.sin(ang)
    tf = t.astype(jnp.float32)
    t1, t2 = tf[..., :half], tf[..., half:]
    return jnp.concatenate([t1 * cos - t2 * sin, t2 * cos + t1 * sin], axis=-1).astype(t.dtype)


def stick_breaking_attention(q, k, v):
    b, h, s, d = q.shape
    nq = s // SB_Q_BLOCK
    scale = 1.0 / math.sqrt(d)
    q_blocks = q.reshape(b, h, nq, SB_Q_BLOCK, d).transpose(2, 0, 1, 3, 4)
    starts = jnp.arange(nq, dtype=jnp.int32) * SB_Q_BLOCK
    kf = k.astype(jnp.float32)
    vf = v.astype(jnp.float32)
    kpos = jnp.arange(s, dtype=jnp.int32)

    def block(args):
        qi, start = args
        z = jnp.einsum('bhqd,bhkd->bhqk', qi.astype(jnp.float32), kf) * scale
        qpos = start + jnp.arange(SB_Q_BLOCK, dtype=jnp.int32)
        mask = kpos[None, :] < qpos[:, None]
        log_1mb = jnp.where(mask, jax.nn.log_sigmoid(-z), 0.0)
        suffix = lax.cumsum(log_1mb, axis=3, reverse=True) - log_1mb
        a = jnp.where(mask, jnp.exp(jax.nn.log_sigmoid(z) + suffix), 0.0)
        return jnp.einsum('bhqk,bhkd->bhqd', a, vf)

    out = lax.map(block, (q_blocks, starts))
    return out.transpose(1, 2, 0, 3, 4).reshape(b, h, s, d).astype(q.dtype)


def moba_attention(q, k, v):
    b, h, s, d = q.shape
    nb = -(-s // MOBA_BLOCK)
    pad = nb * MOBA_BLOCK - s
    kp = jnp.pad(k, ((0, 0), (0, 0), (0, pad), (0, 0)))
    vp = jnp.pad(v, ((0, 0), (0, 0), (0, pad), (0, 0)))
    k_blocks = kp.reshape(b, h, nb, MOBA_BLOCK, d)
    v_blocks = vp.reshape(b, h, nb, MOBA_BLOCK, d)
    k_mean = jnp.mean(k_blocks.astype(jnp.float32), axis=3)
    topk = min(MOBA_TOPK, nb)
    scale = 1.0 / math.sqrt(d)
    nq = s // MOBA_Q_CHUNK
    q_chunks = q.reshape(b, h, nq, MOBA_Q_CHUNK, d).transpose(2, 0, 1, 3, 4)
    starts = jnp.arange(nq, dtype=jnp.int32) * MOBA_Q_CHUNK
    bidx = jnp.arange(b)[:, None, None, None]
    hidx = jnp.arange(h)[None, :, None, None]
    block_ids = jnp.arange(nb, dtype=jnp.int32)

    def chunk(args):
        qi, start = args
        qf = qi.astype(jnp.float32)
        own = start // MOBA_BLOCK
        gate = jnp.einsum('bhqd,bhnd->bhqn', qf, k_mean)
        gate = jnp.where(block_ids < own, gate, -jnp.inf)
        _, idx = lax.top_k(gate, topk)
        sel_valid = jnp.arange(topk, dtype=jnp.int32) < own
        kg = k_blocks[bidx, hidx, idx].astype(jnp.float32)
        vg = v_blocks[bidx, hidx, idx].astype(jnp.float32)
        s_sel = jnp.einsum('bhqd,bhqnkd->bhqnk', qf, kg) * scale
        s_sel = jnp.where(sel_valid[:, None], s_sel, -jnp.inf)
        s_sel = s_sel.reshape(b, h, MOBA_Q_CHUNK, topk * MOBA_BLOCK)
        k_own = lax.dynamic_slice_in_dim(kp, own * MOBA_BLOCK, MOBA_BLOCK, axis=2).astype(jnp.float32)
        v_own = lax.dynamic_slice_in_dim(vp, own * MOBA_BLOCK, MOBA_BLOCK, axis=2).astype(jnp.float32)
        s_own = jnp.einsum('bhqd,bhkd->bhqk', qf, k_own) * scale
        qpos = start + jnp.arange(MOBA_Q_CHUNK, dtype=jnp.int32)
        kpos = own * MOBA_BLOCK + jnp.arange(MOBA_BLOCK, dtype=jnp.int32)
        s_own = jnp.where(kpos[None, :] <= qpos[:, None], s_own, -jnp.inf)
        p = jax.nn.softmax(jnp.concatenate([s_sel, s_own], axis=-1), axis=-1)
        p_sel, p_own = p[..., :topk * MOBA_BLOCK], p[..., topk * MOBA_BLOCK:]
        vg = vg.reshape(b, h, MOBA_Q_CHUNK, topk * MOBA_BLOCK, d)
        return (jnp.einsum('bhqm,bhqmd->bhqd', p_sel, vg)
                + jnp.einsum('bhqk,bhkd->bhqd', p_own, v_own))

    out = lax.map(chunk, (q_chunks, starts))
    return out.transpose(1, 2, 0, 3, 4).reshape(b, h, s, d).astype(q.dtype)


def qk_norm(t, g):
    return rms_norm(t, g)


def hybrid_layer(x, c_act, positions, w_ada, b_ada, g_norm1, w_in, w_sconv, w_cconv, b_cconv,
                 g_cln, b_cln, g_q, g_k, w_out, g_norm2, w_mlp1, w_mlp2):
    G = GROUP_WIDTH
    mod = (c_act @ w_ada + b_ada)[:, None, :]
    shift1, scale1, gate1, shift2, scale2, gate2 = jnp.split(mod, 6, axis=-1)

    hcur = rms_norm(x, g_norm1) * (1.0 + scale1) + shift1
    proj = hcur @ w_in
    sc_b, sc_c, sc_h = proj[..., 0:G], proj[..., G:2 * G], proj[..., 2 * G:3 * G]
    sb_q, sb_k, sb_v = proj[..., 3 * G:4 * G], proj[..., 4 * G:5 * G], proj[..., 5 * G:6 * G]
    mb_q, mb_k, mb_v = proj[..., 6 * G:7 * G], proj[..., 7 * G:8 * G], proj[..., 8 * G:9 * G]
    cf_a, cf_g = proj[..., 9 * G:10 * G], proj[..., 10 * G:11 * G]

    y_a = sc_b * causal_depthwise_conv(sc_c * sc_h, w_sconv)
    y_b = from_heads(stick_breaking_attention(to_heads(sb_q), to_heads(sb_k), to_heads(sb_v)))
    q = apply_rope(qk_norm(to_heads(mb_q), g_q), positions)
    k = apply_rope(qk_norm(to_heads(mb_k), g_k), positions)
    y_c = from_heads(moba_attention(q, k, to_heads(mb_v)))
    u = cf_a * jax.nn.sigmoid(cf_g)
    u = causal_depthwise_conv(u, w_cconv) + b_cconv
    y_d = jax.nn.silu(layer_norm(u, g_cln, b_cln))

    mix = jnp.concatenate([y_a, y_b, y_c, y_d], axis=-1) @ w_out
    x = x + gate1 * mix

    h2 = rms_norm(x, g_norm2) * (1.0 + scale2) + shift2
    f = jnp.square(jax.nn.relu(h2 @ w_mlp1)) @ w_mlp2
    return x + gate2 * f


def setup_inputs(seed: int = 0) -> dict:
    key = jax.random.key(seed)
    ks = jax.random.split(key, 18)
    nrm = jax.random.normal
    f32 = jnp.float32
    G = GROUP_WIDTH
    x = nrm(ks[0], (BATCH, SEQ, D_MODEL), f32)
    c = nrm(ks[1], (BATCH, D_MODEL), f32)
    offsets = jax.random.randint(ks[2], (BATCH, 1), 0, 4096, dtype=jnp.int32)
    positions = (offsets + jnp.arange(SEQ, dtype=jnp.int32)[None, :]).astype(jnp.int32)
    w_ada = nrm(ks[3], (DEPTH, D_MODEL, 6 * D_MODEL), f32) * (0.5 * D_MODEL ** -0.5)
    b_ada = 0.02 * nrm(ks[4], (DEPTH, 6 * D_MODEL), f32)
    g_norm1 = 1.0 + 0.05 * nrm(ks[5], (DEPTH, D_MODEL), f32)
    w_in = nrm(ks[6], (DEPTH, D_MODEL, IN_WIDTH), f32) * D_MODEL ** -0.5
    w_sconv = nrm(ks[7], (DEPTH, SHORT_CONV_WIDTH, G), f32) * SHORT_CONV_WIDTH ** -0.5
    w_cconv = nrm(ks[8], (DEPTH, CONFORMER_CONV_WIDTH, G), f32) * CONFORMER_CONV_WIDTH ** -0.5
    b_cconv = 0.02 * nrm(ks[9], (DEPTH, G), f32)
    g_cln = 1.0 + 0.05 * nrm(ks[10], (DEPTH, G), f32)
    b_cln = 0.02 * nrm(ks[11], (DEPTH, G), f32)
    g_q = 1.0 + 0.05 * nrm(ks[12], (DEPTH, HEAD_DIM), f32)
    g_k = 1.0 + 0.05 * nrm(ks[13], (DEPTH, HEAD_DIM), f32)
    w_out = nrm(ks[14], (DEPTH, MIX_WIDTH, D_MODEL), f32) * MIX_WIDTH ** -0.5
    g_norm2 = 1.0 + 0.05 * nrm(ks[15], (DEPTH, D_MODEL), f32)
    w_mlp1 = nrm(ks[16], (DEPTH, D_MODEL, D_FF), f32) * D_MODEL ** -0.5
    w_mlp2 = nrm(ks[17], (DEPTH, D_FF, D_MODEL), f32) * D_FF ** -0.5
    return {'x': x, 'c': c, 'positions': positions, 'w_ada': w_ada, 'b_ada': b_ada,
            'g_norm1': g_norm1, 'w_in': w_in, 'w_sconv': w_sconv, 'w_cconv': w_cconv,
            'b_cconv': b_cconv, 'g_cln': g_cln, 'b_cln': b_cln, 'g_q': g_q, 'g_k': g_k,
            'w_out': w_out, 'g_norm2': g_norm2, 'w_mlp1': w_mlp1, 'w_mlp2': w_mlp2}


def reference(x, c, positions, w_ada, b_ada, g_norm1, w_in, w_sconv, w_cconv, b_cconv,
              g_cln, b_cln, g_q, g_k, w_out, g_norm2, w_mlp1, w_mlp2):
    c_act = jax.nn.silu(c)
    for l in range(DEPTH):
        x = hybrid_layer(x, c_act, positions, w_ada[l], b_ada[l], g_norm1[l], w_in[l],
                         w_sconv[l], w_cconv[l], b_cconv[l], g_cln[l], b_cln[l],
                         g_q[l], g_k[l], w_out[l], g_norm2[l], w_mlp1[l], w_mlp2[l])
    return x
```

```python
import functools
import math

import jax
import jax.numpy as jnp
from jax import lax
from jax.experimental import pallas as pl
from jax.experimental.pallas import tpu as pltpu

F32 = jnp.float32
BF16 = jnp.bfloat16
HIGHEST = lax.Precision.HIGHEST

D_MODEL = 1024
GROUP = 256
HEAD_DIM = 64
N_HEADS = GROUP // HEAD_DIM
IN_WIDTH = 11 * GROUP
D_FF = 4 * D_MODEL
SHORT_K = 3
CONF_K = 31
SB_BLOCK = 128
MOBA_BLOCK = 256
MOBA_TOPK = 3
ROPE_THETA = 10000.0
EPS = 1e-6
LANES = 128
AUG = 128
GATE_LANE0 = HEAD_DIM
GATE_WIDTH = 32
MASK_BIG = 2.0 ** 100
SB_EXIT = -90.0
VMEM_LIMIT = 56 * 1024 * 1024


def _cparams(sem):
    return pltpu.CompilerParams(dimension_semantics=sem, vmem_limit_bytes=VMEM_LIMIT)


def _const_spec(shape):
    n = len(shape)
    return pl.BlockSpec(shape, lambda *_: (0,) * n, pipeline_mode=pl.Buffered(1))


def _ada_kernel(c_ref, w_ref, b_ref, o_ref):
    c = c_ref[...]
    ca = c * jax.nn.sigmoid(c)
    o_ref[0] = jnp.dot(ca, w_ref[0], precision=HIGHEST, preferred_element_type=F32) + b_ref[0]


def _ada_call(c, w_ada, b_ada):
    depth, d, n = w_ada.shape
    b = c.shape[0]
    tn = 1536
    return pl.pallas_call(
        _ada_kernel,
        grid=(depth, n // tn),
        in_specs=[pl.BlockSpec((b, d), lambda l, j: (0, 0)),
                  pl.BlockSpec((1, d, tn), lambda l, j: (l, 0, j)),
                  pl.BlockSpec((1, 1, tn), lambda l, j: (l, 0, j))],
        out_specs=pl.BlockSpec((1, b, tn), lambda l, j: (l, 0, j)),
        out_shape=jax.ShapeDtypeStruct((depth, b, n), F32),
        compiler_params=_cparams(("arbitrary", "arbitrary")),
        name="ada_mod",
    )(c, w_ada, b_ada.reshape(depth, 1, n))


def _rope_kernel(pos_ref, cos_ref, sin_ref):
    pos = pos_ref[0].astype(F32)
    lane = lax.broadcasted_iota(jnp.int32, (1, LANES), 1)
    half = HEAD_DIM // 2
    idx = (lane % half).astype(F32)
    inv_freq = jnp.exp((-math.log(ROPE_THETA) * idx) / half)
    ang = pos * inv_freq
    sign = jnp.where((lane % HEAD_DIM) < half, -1.0, 1.0)
    cos_ref[0] = jnp.cos(ang)
    sin_ref[0] = jnp.sin(ang) * sign


def _rope_call(positions):
    b, s = positions.shape
    ts = 1024
    spec = pl.BlockSpec((1, ts, LANES), lambda i, j: (i, j, 0))
    return pl.pallas_call(
        _rope_kernel,
        grid=(b, s // ts),
        in_specs=[pl.BlockSpec((1, ts, 1), lambda i, j: (i, j, 0))],
        out_specs=[spec, spec],
        out_shape=[jax.ShapeDtypeStruct((b, s, LANES), F32)] * 2,
        compiler_params=_cparams(("arbitrary", "arbitrary")),
        name="rope_table",
    )(positions.reshape(b, s, 1))


def _inproj_kernel(x_ref, sh_ref, sc_ref, g1_ref, w_ref, cos_ref, sin_ref, gq_ref, gk_ref, pk_ref,
                   eh_ref, scb_ref, scch_ref, sbq_ref, sbk_ref, sbv_ref, mbq_ref, kaug_ref, vaug_ref,
                   kmean_ref, cfu_ref):
    tm = x_ref.shape[1]
    g = GROUP
    x = x_ref[0]
    ms = jnp.mean(x * x, axis=-1, keepdims=True)
    h = x * lax.rsqrt(ms + EPS) * g1_ref[...]
    h = h * (1.0 + sc_ref[0]) + sh_ref[0]
    hb = h.astype(BF16)

    def proj(c0, n):
        return jnp.dot(hb, w_ref[:, c0:c0 + n], preferred_element_type=F32)

    p = proj(0, 3 * g)
    scb_ref[0] = p[:, :g]
    scch_ref[0] = p[:, g:2 * g] * p[:, 2 * g:]

    p = proj(3 * g, 3 * g)
    sbq_ref[0] = (p[:, :g] * (1.0 / math.sqrt(HEAD_DIM))).astype(BF16)
    sbk_ref[0] = p[:, g:2 * g].astype(BF16)
    sbv_ref[0] = p[:, 2 * g:].astype(BF16)

    p = proj(6 * g, 3 * g)
    cos_t = cos_ref[0]
    sin_t = sin_ref[0]
    lane = lax.broadcasted_iota(jnp.int32, (tm, LANES), 1)
    first_half = (lane % HEAD_DIM) < (HEAD_DIM // 2)
    half = HEAD_DIM // 2

    def norm_rope(t, gain):
        ss = jnp.dot(t * t, eh_ref[...], precision=HIGHEST, preferred_element_type=F32)
        y = t * lax.rsqrt(ss * (1.0 / HEAD_DIM) + EPS) * gain
        out = []
        for c0 in range(0, g, LANES):
            yh = y[:, c0:c0 + LANES]
            partner = jnp.where(first_half, pltpu.roll(yh, LANES - half, 1), pltpu.roll(yh, half, 1))
            out.append(yh * cos_t + partner * sin_t)
        return jnp.concatenate(out, axis=1)

    q = norm_rope(p[:, :g], gq_ref[...])
    k = norm_rope(p[:, g:2 * g], gk_ref[...])
    mbq_ref[0] = q
    for r in range(tm // MOBA_BLOCK):
        kmean_ref[0, 0, r:r + 1, :] = jnp.mean(k[r * MOBA_BLOCK:(r + 1) * MOBA_BLOCK], axis=0, keepdims=True)

    lane_a = lax.broadcasted_iota(jnp.int32, (tm, N_HEADS * AUG), 1) % AUG
    row = lax.broadcasted_iota(jnp.int32, (tm, N_HEADS * AUG), 0)
    blk = pl.program_id(1) * (tm // MOBA_BLOCK) + jnp.right_shift(row, int(math.log2(MOBA_BLOCK)))
    kaug = jnp.dot(k.astype(BF16), pk_ref[...], preferred_element_type=F32)
    kaug_ref[0] = jnp.where(lane_a - GATE_LANE0 == blk, 1.0, kaug).astype(BF16)
    vaug = jnp.dot(p[:, 2 * g:].astype(BF16), pk_ref[...], preferred_element_type=F32)
    vaug_ref[0] = jnp.where(lane_a >= HEAD_DIM, 1.0, vaug).astype(BF16)

    p = proj(9 * g, 2 * g)
    cfu_ref[0] = p[:, :g] * jax.nn.sigmoid(p[:, g:])


def _inproj_call(x, shift, scale, g1, w_in, cos_t, sin_t, gq, gk, pk, eh, tm):
    b, s, d = x.shape
    g = GROUP
    nbt = tm // MOBA_BLOCK
    row = lambda w: pl.BlockSpec((1, tm, w), lambda i, j: (i, j, 0))
    mod = pl.BlockSpec((1, 1, d), lambda i, j: (i, 0, 0))
    f32o = lambda w: jax.ShapeDtypeStruct((b, s, w), F32)
    bf16o = lambda w: jax.ShapeDtypeStruct((b, s, w), BF16)
    return pl.pallas_call(
        _inproj_kernel,
        grid=(b, s // tm),
        in_specs=[row(d), mod, mod, _const_spec((1, d)), _const_spec(w_in.shape), row(LANES), row(LANES),
                  _const_spec((1, g)), _const_spec((1, g)), _const_spec(pk.shape), _const_spec(eh.shape)],
        out_specs=[row(g), row(g), row(g), row(g), row(g), row(g), row(N_HEADS * AUG), row(N_HEADS * AUG),
                   pl.BlockSpec((1, 1, nbt, g), lambda i, j: (i, j, 0, 0)), row(g)],
        out_shape=[f32o(g), f32o(g), bf16o(g), bf16o(g), bf16o(g), f32o(g), bf16o(N_HEADS * AUG),
                   bf16o(N_HEADS * AUG), jax.ShapeDtypeStruct((b, s // tm, nbt, g), F32), f32o(g)],
        compiler_params=_cparams(("arbitrary", "arbitrary")),
        name="in_proj",
    )(x, shift, scale, g1, w_in, cos_t, sin_t, gq, gk, pk, eh)


CONV_HALO = 32
SHORT_HALO = 8
CONV_CHUNK = 64


def _conv_kernel(scb_ref, scch_ref, cfu_ref, ws_ref, wc_ref, bc_ref, gl_ref, bl_ref, ya_ref, yd_ref,
                 exts, extc):
    tc = scb_ref.shape[1]
    j = pl.program_id(1)

    @pl.when(j == 0)
    def _():
        exts[0:SHORT_HALO] = jnp.zeros((SHORT_HALO, GROUP), F32)
        extc[0:CONV_HALO] = jnp.zeros((CONV_HALO, GROUP), F32)

    @pl.when(j > 0)
    def _():
        exts[0:SHORT_HALO] = exts[tc:tc + SHORT_HALO]
        extc[0:CONV_HALO] = extc[tc:tc + CONV_HALO]

    exts[SHORT_HALO:SHORT_HALO + tc] = scch_ref[0]
    extc[CONV_HALO:CONV_HALO + tc] = cfu_ref[0]

    for c0 in range(0, tc, CONV_CHUNK):
        acc = jnp.zeros((CONV_CHUNK, GROUP), F32)
        for kk in range(SHORT_K):
            o = SHORT_HALO - (SHORT_K - 1) + kk + c0
            acc = acc + ws_ref[kk:kk + 1, :] * exts[o:o + CONV_CHUNK]
        ya_ref[0, c0:c0 + CONV_CHUNK] = (scb_ref[0, c0:c0 + CONV_CHUNK] * acc).astype(BF16)

        acc = jnp.zeros((CONV_CHUNK, GROUP), F32)
        for kk in range(CONF_K):
            o = CONV_HALO - (CONF_K - 1) + kk + c0
            acc = acc + wc_ref[kk:kk + 1, :] * extc[o:o + CONV_CHUNK]
        u = acc + bc_ref[...]
        mu = jnp.mean(u, axis=-1, keepdims=True)
        var = jnp.mean(jnp.square(u - mu), axis=-1, keepdims=True)
        y = (u - mu) * lax.rsqrt(var + EPS) * gl_ref[...] + bl_ref[...]
        yd_ref[0, c0:c0 + CONV_CHUNK] = (y * jax.nn.sigmoid(y)).astype(BF16)


def _conv_call(scb, scch, cfu, ws, wc, bc, gl, bl, tc):
    b, s, g = scb.shape
    row = pl.BlockSpec((1, tc, g), lambda i, j: (i, j, 0))
    out = jax.ShapeDtypeStruct((b, s, g), BF16)
    return pl.pallas_call(
        _conv_kernel,
        grid=(b, s // tc),
        in_specs=[row, row, row, _const_spec(ws.shape), _const_spec(wc.shape), _const_spec((1, g)),
                  _const_spec((1, g)), _const_spec((1, g))],
        out_specs=[row, row],
        out_shape=[out, out],
        scratch_shapes=[pltpu.VMEM((tc + SHORT_HALO, g), F32), pltpu.VMEM((tc + CONV_HALO, g), F32)],
        compiler_params=_cparams(("arbitrary", "arbitrary")),
        name="causal_convs",
    )(scb, scch, cfu, ws, wc, bc, gl, bl)


def _softplus(z):
    return jnp.maximum(z, 0.0) + jnp.log1p(jnp.exp(-jnp.abs(z)))


def _suffix_sum(l, upper):
    hi = l.astype(BF16)
    lo = (l - hi.astype(F32)).astype(BF16)
    return (jnp.dot(hi, upper, preferred_element_type=F32) + jnp.dot(lo, upper, preferred_element_type=F32))


def _sb_kernel(q_ref, k_ref, v_ref, o_ref, r_ref, acc_ref):
    t = SB_BLOCK
    i = pl.program_id(1)
    row = lax.broadcasted_iota(jnp.int32, (t, t), 0)
    col = lax.broadcasted_iota(jnp.int32, (t, t), 1)
    upper = (row >= col).astype(BF16)
    strict = col < row
    low_head = col < HEAD_DIM

    def qmask(h):
        pair = q_ref[0, :, (h // 2) * LANES:(h // 2 + 1) * LANES]
        keep = low_head if h % 2 == 0 else jnp.logical_not(low_head)
        return jnp.where(keep, pair, jnp.zeros_like(pair))

    def kv(ref, start, h):
        return ref[0, pl.ds(start, t), (h // 2) * LANES:(h // 2 + 1) * LANES]

    def scores(h, start):
        return lax.dot_general(qmask(h), kv(k_ref, start, h), (((1,), (1,)), ((), ())),
                               preferred_element_type=F32)

    start = pl.multiple_of(i * t, t)
    for h in range(N_HEADS):
        z = scores(h, start)
        l = jnp.where(strict, -_softplus(z), 0.0)
        c = _suffix_sum(l, upper)
        a = jnp.where(strict, jnp.exp(z + c), 0.0)
        acc_ref[h] = jnp.dot(a.astype(BF16), kv(v_ref, start, h), preferred_element_type=F32)
        r_ref[h] = jnp.broadcast_to(c[:, 0:1], (t, t))

    def cond(carry):
        j, rmax = carry
        return jnp.logical_and(j >= 0, rmax > SB_EXIT)

    def body(carry):
        j, _ = carry
        start = pl.multiple_of(j * t, t)
        rmax = jnp.float32(-jnp.inf)
        for h in range(N_HEADS):
            z = scores(h, start)
            r = r_ref[h]
            c = _suffix_sum(-_softplus(z), upper)
            a = jnp.exp(z + c + r)
            acc_ref[h] = acc_ref[h] + jnp.dot(a.astype(BF16), kv(v_ref, start, h),
                                              preferred_element_type=F32)
            r_new = r + jnp.broadcast_to(c[:, 0:1], (t, t))
            r_ref[h] = r_new
            rmax = jnp.maximum(rmax, jnp.max(r_new))
        return j - 1, rmax

    lax.while_loop(cond, body, (i - 1, jnp.float32(0.0)))

    for p in range(N_HEADS // 2):
        o_ref[0, :, p * LANES:(p + 1) * LANES] = jnp.where(low_head, acc_ref[2 * p], acc_ref[2 * p + 1]).astype(BF16)


def _sb_call(q, k, v):
    b, s, g = q.shape
    t = SB_BLOCK
    seq = pl.BlockSpec((1, s, g), lambda i, j: (i, 0, 0))
    tile = pl.BlockSpec((1, t, g), lambda i, j: (i, j, 0))
    return pl.pallas_call(
        _sb_kernel,
        grid=(b, s // t),
        in_specs=[tile, seq, seq],
        out_specs=tile,
        out_shape=jax.ShapeDtypeStruct((b, s, g), BF16),
        scratch_shapes=[pltpu.VMEM((N_HEADS, t, t), F32), pltpu.VMEM((N_HEADS, t, t), F32)],
        compiler_params=_cparams(("arbitrary", "arbitrary")),
        name="stick_breaking",
    )(q, k, v)


def _moba_kernel(q_ref, kaug_ref, vaug_ref, km_ref, pq_ref, po_ref, o_ref, qa_ref, m_ref, acc_ref):
    t = MOBA_BLOCK
    i = pl.program_id(1)
    lane = lax.broadcasted_iota(jnp.int32, (t, AUG), 1)
    gate_lanes = jnp.logical_and(lane >= GATE_LANE0, lane < GATE_LANE0 + i)
    own_lane = lane == GATE_LANE0 + i
    row = lax.broadcasted_iota(jnp.int32, (t, t), 0)
    col = lax.broadcasted_iota(jnp.int32, (t, t), 1)
    causal = col <= row
    nt = (((1,), (1,)), ((), ()))

    def kblk(ref, start, h):
        return ref[0, pl.ds(start, t), h * AUG:(h + 1) * AUG]

    qf = q_ref[0]
    qb = (qf * (1.0 / math.sqrt(HEAD_DIM))).astype(BF16)
    own_start = pl.multiple_of(i * t, t)
    for h in range(N_HEADS):
        gate = jnp.dot(qf, km_ref[0, h], precision=HIGHEST, preferred_element_type=F32)
        gsc = jnp.where(gate_lanes, gate, -jnp.inf)
        sel = own_lane
        for _ in range(MOBA_TOPK):
            mx = jnp.max(gsc, axis=-1, keepdims=True)
            first = jnp.min(jnp.where(gsc == mx, lane, 4 * AUG), axis=-1, keepdims=True)
            pick = jnp.logical_and(lane == first, mx > -jnp.inf)
            sel = jnp.logical_or(sel, pick)
            gsc = jnp.where(pick, -jnp.inf, gsc)
        in_gate = jnp.logical_and(lane >= GATE_LANE0, lane < GATE_LANE0 + GATE_WIDTH)
        bias = jnp.where(jnp.logical_and(in_gate, jnp.logical_not(sel)), -MASK_BIG, 0.0)
        qa = (jnp.dot(qb, pq_ref[h], preferred_element_type=F32) + bias).astype(BF16)
        qa_ref[h] = qa

        s = lax.dot_general(qa, kblk(kaug_ref, own_start, h), nt, preferred_element_type=F32)
        s = jnp.where(causal, s, -jnp.inf)
        m = jnp.max(s, axis=-1, keepdims=True)
        p = jnp.exp(s - m)
        acc_ref[h] = jnp.dot(p.astype(BF16), kblk(vaug_ref, own_start, h), preferred_element_type=F32)
        m_ref[h] = jnp.broadcast_to(m, (t, AUG))

    def body(j, carry):
        start = pl.multiple_of(j * t, t)
        for h in range(N_HEADS):
            s = lax.dot_general(qa_ref[h], kblk(kaug_ref, start, h), nt, preferred_element_type=F32)
            m_old = m_ref[h]
            m_new = jnp.maximum(m_old, jnp.max(s, axis=-1, keepdims=True))
            alpha = jnp.exp(m_old - m_new)
            p = jnp.exp(s - jnp.concatenate([m_new, m_new], axis=1))
            acc_ref[h] = acc_ref[h] * alpha + jnp.dot(p.astype(BF16), kblk(vaug_ref, start, h),
                                                      preferred_element_type=F32)
            m_ref[h] = m_new
        return carry

    lax.fori_loop(0, i, body, 0)

    out = jnp.zeros((t, GROUP), F32)
    for h in range(N_HEADS):
        acc = acc_ref[h]
        denom = pltpu.roll(acc, HEAD_DIM, 1)
        y = jnp.where(lane < HEAD_DIM, acc / denom, 0.0).astype(BF16)
        out = out + jnp.dot(y, po_ref[h], preferred_element_type=F32)
    o_ref[0] = out.astype(BF16)


def _moba_call(q, kaug, vaug, km, pq, po):
    b, s, g = q.shape
    t = MOBA_BLOCK
    seq = pl.BlockSpec((1, s, N_HEADS * AUG), lambda i, j: (i, 0, 0))
    tile = pl.BlockSpec((1, t, g), lambda i, j: (i, j, 0))
    return pl.pallas_call(
        _moba_kernel,
        grid=(b, s // t),
        in_specs=[tile, seq, seq, pl.BlockSpec((1,) + km.shape[1:], lambda i, j: (i, 0, 0, 0)),
                  _const_spec(pq.shape), _const_spec(po.shape)],
        out_specs=tile,
        out_shape=jax.ShapeDtypeStruct((b, s, g), BF16),
        scratch_shapes=[pltpu.VMEM((N_HEADS, t, AUG), BF16), pltpu.VMEM((N_HEADS, t, AUG), F32),
                        pltpu.VMEM((N_HEADS, t, AUG), F32)],
        compiler_params=_cparams(("arbitrary", "arbitrary")),
        name="moba",
    )(q, kaug, vaug, km, pq, po)


MLP_CHUNK = 512


def _outmlp_kernel(x_ref, ya_ref, yb_ref, yc_ref, yd_ref, g1_ref, sh_ref, sc_ref, g2_ref, gn_ref, wo_ref,
                   w1_ref, w2_ref, o_ref):
    g = GROUP
    mix = jnp.dot(ya_ref[0], wo_ref[0:g], preferred_element_type=F32)
    mix = mix + jnp.dot(yb_ref[0], wo_ref[g:2 * g], preferred_element_type=F32)
    mix = mix + jnp.dot(yc_ref[0], wo_ref[2 * g:3 * g], preferred_element_type=F32)
    mix = mix + jnp.dot(yd_ref[0], wo_ref[3 * g:4 * g], preferred_element_type=F32)
    x1 = x_ref[0] + g1_ref[0] * mix
    ms = jnp.mean(x1 * x1, axis=-1, keepdims=True)
    h = x1 * lax.rsqrt(ms + EPS) * gn_ref[...]
    hb = (h * (1.0 + sc_ref[0]) + sh_ref[0]).astype(BF16)
    acc = jnp.zeros(x1.shape, F32)
    for f0 in range(0, D_FF, MLP_CHUNK):
        a = jnp.maximum(jnp.dot(hb, w1_ref[:, f0:f0 + MLP_CHUNK], preferred_element_type=F32), 0.0)
        acc = acc + jnp.dot((a * a).astype(BF16), w2_ref[f0:f0 + MLP_CHUNK, :], preferred_element_type=F32)
    o_ref[0] = x1 + g2_ref[0] * acc


def _outmlp_call(x, ya, yb, yc, yd, gate1, shift2, scale2, gate2, gn2, wo, w1, w2, tm):
    b, s, d = x.shape
    row = lambda w: pl.BlockSpec((1, tm, w), lambda i, j: (i, j, 0))
    mod = pl.BlockSpec((1, 1, d), lambda i, j: (i, 0, 0))
    return pl.pallas_call(
        _outmlp_kernel,
        grid=(b, s // tm),
        in_specs=[row(d), row(GROUP), row(GROUP), row(GROUP), row(GROUP), mod, mod, mod, mod,
                  _const_spec((1, d)), _const_spec(wo.shape), _const_spec(w1.shape), _const_spec(w2.shape)],
        out_specs=row(d),
        out_shape=jax.ShapeDtypeStruct((b, s, d), F32),
        compiler_params=_cparams(("arbitrary", "arbitrary")),
        name="out_mlp",
    )(x, ya, yb, yc, yd, gate1, shift2, scale2, gate2, gn2, wo, w1, w2)


def _placement_matrices():
    src = jnp.arange(GROUP)
    dst = (src // HEAD_DIM) * AUG + src % HEAD_DIM
    pk = jnp.zeros((GROUP, N_HEADS * AUG), BF16).at[src, dst].set(1.0)
    head = jnp.arange(N_HEADS)[:, None, None]
    r = jnp.arange(GROUP)[None, :, None]
    c = jnp.arange(AUG)[None, None, :]
    pq = ((r // HEAD_DIM == head) & (c == r % HEAD_DIM)).astype(BF16)
    po = jnp.transpose(pq, (0, 2, 1))
    eh = (src[:, None] // HEAD_DIM == src[None, :] // HEAD_DIM).astype(F32)
    return pk, pq, po, eh


def _gate_operand(kmean):
    b, nb, _ = kmean.shape
    km = kmean.reshape(b, nb, N_HEADS, HEAD_DIM).transpose(0, 2, 3, 1)
    out = jnp.zeros((b, N_HEADS, N_HEADS, HEAD_DIM, AUG), F32)
    for h in range(N_HEADS):
        out = out.at[:, h, h, :, GATE_LANE0:GATE_LANE0 + nb].set(km[:, h])
    return out.reshape(b, N_HEADS, GROUP, AUG)


def _pick_tile(s, pref):
    t = pref
    while s % t:
        t //= 2
    return t


def kernel(x, c, positions, w_ada, b_ada, g_norm1, w_in, w_sconv, w_cconv, b_cconv, g_cln, b_cln, g_q, g_k,
           w_out, g_norm2, w_mlp1, w_mlp2):
    b, s, d = x.shape
    depth = w_ada.shape[0]
    assert d == D_MODEL and s % MOBA_BLOCK == 0 and s // MOBA_BLOCK <= GATE_WIDTH
    tm = _pick_tile(s, 512)
    pk, pq, po, eh = _placement_matrices()
    mod = _ada_call(c, w_ada, b_ada).reshape(depth, b, 6, 1, d)
    cos_t, sin_t = _rope_call(positions)
    tile4 = lambda v: jnp.tile(v, N_HEADS).reshape(1, GROUP)
    for l in range(depth):
        shift1, scale1, gate1, shift2, scale2, gate2 = (mod[l, :, i] for i in range(6))
        scb, scch, sbq, sbk, sbv, mbq, kaug, vaug, kmean, cfu = _inproj_call(
            x, shift1, scale1, g_norm1[l].reshape(1, d), w_in[l].astype(BF16), cos_t, sin_t,
            tile4(g_q[l]), tile4(g_k[l]), pk, eh, tm)
        ya, yd = _conv_call(scb, scch, cfu, w_sconv[l], w_cconv[l], b_cconv[l].reshape(1, GROUP),
                            g_cln[l].reshape(1, GROUP), b_cln[l].reshape(1, GROUP), tm)
        yb = _sb_call(sbq, sbk, sbv)
        yc = _moba_call(mbq, kaug, vaug, _gate_operand(kmean.reshape(b, s // MOBA_BLOCK, GROUP)), pq, po)
        x = _outmlp_call(x, ya, yb, yc, yd, gate1, shift2, scale2, gate2, g_norm2[l].reshape(1, d),
                         w_out[l].astype(BF16), w_mlp1[l].astype(BF16), w_mlp2[l].astype(BF16), tm)
    return x
```

```python
import functools
import math

import jax
import jax.numpy as jnp
from jax import lax
from jax.experimental import pallas as pl
from jax.experimental.pallas import tpu as pltpu

F32 = jnp.float32
BF16 = jnp.bfloat16
HIGHEST = lax.Precision.HIGHEST

D_MODEL = 1024
GROUP = 256
HEAD_DIM = 64
N_HEADS = GROUP // HEAD_DIM
IN_WIDTH = 11 * GROUP
D_FF = 4 * D_MODEL
SHORT_K = 3
CONF_K = 31
SB_BLOCK = 128
MOBA_BLOCK = 256
MOBA_TOPK = 3
MOBA_KV_TILE = 512
ROPE_THETA = 10000.0
EPS = 1e-6
LANES = 128
SUBLANES = 8
AUG = 128
GATE_LANE0 = HEAD_DIM
GATE_WIDTH = 32
MASK_BIG = 2.0 ** 100
SB_EXIT = -90.0
VMEM_LIMIT = 56 * 1024 * 1024


def _log2(n):
    assert n & (n - 1) == 0
    return n.bit_length() - 1


def _cparams(sem):
    return pltpu.CompilerParams(dimension_semantics=sem, vmem_limit_bytes=VMEM_LIMIT)


def _split_bf16(a):
    hi = a.astype(BF16)
    return hi, (a - hi.astype(F32)).astype(BF16)


def _const_spec(shape):
    n = len(shape)
    return pl.BlockSpec(shape, lambda *_: (0,) * n, pipeline_mode=pl.Buffered(1))


def _ada_kernel(c_ref, w_ref, b_ref, o_ref):
    c = c_ref[...]
    ca = c * jax.nn.sigmoid(c)
    o_ref[0] = jnp.dot(ca, w_ref[0], precision=HIGHEST, preferred_element_type=F32) + b_ref[0]


def _ada_call(c, w_ada, b_ada):
    depth, d, n = w_ada.shape
    b = c.shape[0]
    tn = 1536
    return pl.pallas_call(
        _ada_kernel,
        grid=(depth, n // tn),
        in_specs=[pl.BlockSpec((b, d), lambda l, j: (0, 0)),
                  pl.BlockSpec((1, d, tn), lambda l, j: (l, 0, j)),
                  pl.BlockSpec((1, 1, tn), lambda l, j: (l, 0, j))],
        out_specs=pl.BlockSpec((1, b, tn), lambda l, j: (l, 0, j)),
        out_shape=jax.ShapeDtypeStruct((depth, b, n), F32),
        compiler_params=_cparams(("arbitrary", "arbitrary")),
        name="ada_mod",
    )(c, w_ada, b_ada.reshape(depth, 1, n))


def _rope_kernel(pos_ref, cos_ref, sin_ref):
    pos = pos_ref[0].astype(F32)
    lane = lax.broadcasted_iota(jnp.int32, (1, LANES), 1)
    half = HEAD_DIM // 2
    idx = (lane % half).astype(F32)
    inv_freq = jnp.exp((-math.log(ROPE_THETA) * idx) / half)
    ang = pos * inv_freq
    sign = jnp.where((lane % HEAD_DIM) < half, -1.0, 1.0)
    cos_ref[0] = jnp.cos(ang)
    sin_ref[0] = jnp.sin(ang) * sign


def _rope_call(positions):
    b, s = positions.shape
    ts = 1024
    spec = pl.BlockSpec((1, ts, LANES), lambda i, j: (i, j, 0))
    return pl.pallas_call(
        _rope_kernel,
        grid=(b, s // ts),
        in_specs=[pl.BlockSpec((1, ts, 1), lambda i, j: (i, j, 0))],
        out_specs=[spec, spec],
        out_shape=[jax.ShapeDtypeStruct((b, s, LANES), F32)] * 2,
        compiler_params=_cparams(("arbitrary", "arbitrary")),
        name="rope_table",
    )(positions.reshape(b, s, 1))


def _inproj_kernel(x_ref, sh_ref, sc_ref, g1_ref, w_ref, cos_ref, sin_ref, gq_ref, gk_ref, pk_ref,
                   eh_ref, scb_ref, scch_ref, sbq_ref, sbk_ref, sbv_ref, mbq_ref, kaug_ref, vaug_ref,
                   kmean_ref, cfu_ref):
    tm = x_ref.shape[1]
    g = GROUP
    x = x_ref[0]
    ms = jnp.mean(x * x, axis=-1, keepdims=True)
    h = x * lax.rsqrt(ms + EPS) * g1_ref[...]
    h = h * (1.0 + sc_ref[0]) + sh_ref[0]
    hb = h.astype(BF16)

    def proj(c0, n):
        return jnp.dot(hb, w_ref[:, c0:c0 + n], preferred_element_type=F32)

    p = proj(0, 3 * g)
    scb_ref[0] = p[:, :g]
    scch_ref[0] = p[:, g:2 * g] * p[:, 2 * g:]

    p = proj(3 * g, 3 * g)
    sbq_ref[0] = (p[:, :g] * (1.0 / math.sqrt(HEAD_DIM))).astype(BF16)
    sbk_ref[0] = p[:, g:2 * g].astype(BF16)
    sbv_ref[0] = p[:, 2 * g:].astype(BF16)

    p = proj(6 * g, 3 * g)
    cos_t = cos_ref[0]
    sin_t = sin_ref[0]
    lane = lax.broadcasted_iota(jnp.int32, (tm, LANES), 1)
    first_half = (lane % HEAD_DIM) < (HEAD_DIM // 2)
    half = HEAD_DIM // 2

    def norm_rope(t, gain):
        sq_hi, sq_lo = _split_bf16(t * t)
        ss = (jnp.dot(sq_hi, eh_ref[...], preferred_element_type=F32)
              + jnp.dot(sq_lo, eh_ref[...], preferred_element_type=F32))
        y = t * lax.rsqrt(ss * (1.0 / HEAD_DIM) + EPS) * gain
        out = []
        for c0 in range(0, g, LANES):
            yh = y[:, c0:c0 + LANES]
            partner = jnp.where(first_half, pltpu.roll(yh, LANES - half, 1), pltpu.roll(yh, half, 1))
            out.append(yh * cos_t + partner * sin_t)
        return jnp.concatenate(out, axis=1)

    q = norm_rope(p[:, :g], gq_ref[...])
    k = norm_rope(p[:, g:2 * g], gk_ref[...])
    mbq_ref[0] = q
    for r in range(tm // MOBA_BLOCK):
        kmean_ref[0, 0, r:r + 1, :] = jnp.mean(k[r * MOBA_BLOCK:(r + 1) * MOBA_BLOCK], axis=0, keepdims=True)

    lane_a = lax.broadcasted_iota(jnp.int32, (tm, N_HEADS * AUG), 1) % AUG
    row = lax.broadcasted_iota(jnp.int32, (tm, N_HEADS * AUG), 0)
    blk = pl.program_id(1) * (tm // MOBA_BLOCK) + jnp.right_shift(row, _log2(MOBA_BLOCK))
    kaug = jnp.dot(k.astype(BF16), pk_ref[...], preferred_element_type=F32)
    kaug_ref[0] = jnp.where(lane_a - GATE_LANE0 == blk, 1.0, kaug).astype(BF16)
    vaug = jnp.dot(p[:, 2 * g:].astype(BF16), pk_ref[...], preferred_element_type=F32)
    vaug_ref[0] = jnp.where(lane_a >= HEAD_DIM, 1.0, vaug).astype(BF16)

    p = proj(9 * g, 2 * g)
    cfu_ref[0] = p[:, :g] * jax.nn.sigmoid(p[:, g:])


def _inproj_call(x, shift, scale, g1, w_in, cos_t, sin_t, gq, gk, pk, eh, tm):
    b, s, d = x.shape
    g = GROUP
    nbt = tm // MOBA_BLOCK
    row = lambda w: pl.BlockSpec((1, tm, w), lambda i, j: (i, j, 0))
    mod = pl.BlockSpec((1, 1, d), lambda i, j: (i, 0, 0))
    f32o = lambda w: jax.ShapeDtypeStruct((b, s, w), F32)
    bf16o = lambda w: jax.ShapeDtypeStruct((b, s, w), BF16)
    return pl.pallas_call(
        _inproj_kernel,
        grid=(b, s // tm),
        in_specs=[row(d), mod, mod, _const_spec((1, d)), _const_spec(w_in.shape), row(LANES), row(LANES),
                  _const_spec((1, g)), _const_spec((1, g)), _const_spec(pk.shape), _const_spec(eh.shape)],
        out_specs=[row(g), row(g), row(g), row(g), row(g), row(g), row(N_HEADS * AUG), row(N_HEADS * AUG),
                   pl.BlockSpec((1, 1, nbt, g), lambda i, j: (i, j, 0, 0)), row(g)],
        out_shape=[f32o(g), f32o(g), bf16o(g), bf16o(g), bf16o(g), f32o(g), bf16o(N_HEADS * AUG),
                   bf16o(N_HEADS * AUG), jax.ShapeDtypeStruct((b, s // tm, nbt, g), F32), f32o(g)],
        compiler_params=_cparams(("arbitrary", "arbitrary")),
        name="in_proj",
    )(x, shift, scale, g1, w_in, cos_t, sin_t, gq, gk, pk, eh)


CONV_HALO = 32
SHORT_HALO = 8
CONV_CHUNK = 64


def _conv_kernel(scb_ref, scch_ref, cfu_ref, ws_ref, wc_ref, bc_ref, gl_ref, bl_ref, ya_ref, yd_ref,
                 exts, extc, shc):
    tc = scb_ref.shape[1]
    j = pl.program_id(1)

    @pl.when(j == 0)
    def _():
        exts[0:SHORT_HALO] = jnp.zeros((SHORT_HALO, GROUP), F32)
        extc[0:CONV_HALO] = jnp.zeros((CONV_HALO, GROUP), F32)

    @pl.when(j > 0)
    def _():
        exts[0:SHORT_HALO] = exts[tc:tc + SHORT_HALO]
        extc[0:CONV_HALO] = extc[tc:tc + CONV_HALO]

    exts[SHORT_HALO:SHORT_HALO + tc] = scch_ref[0]
    extc[CONV_HALO:CONV_HALO + tc] = cfu_ref[0]

    n_ext = tc + CONV_HALO
    for r in range(SUBLANES):
        n = n_ext - (SUBLANES if r else 0)
        shc[r, 0:n] = extc[r:r + n]

    for c0 in range(0, tc, CONV_CHUNK):
        acc = jnp.zeros((CONV_CHUNK, GROUP), F32)
        for kk in range(SHORT_K):
            o = SHORT_HALO - (SHORT_K - 1) + kk + c0
            acc = acc + ws_ref[kk:kk + 1, :] * exts[o:o + CONV_CHUNK]
        ya_ref[0, c0:c0 + CONV_CHUNK] = (scb_ref[0, c0:c0 + CONV_CHUNK] * acc).astype(BF16)

        acc = jnp.zeros((CONV_CHUNK, GROUP), F32)
        for kk in range(CONF_K):
            o = CONV_HALO - (CONF_K - 1) + kk
            a0 = o - o % SUBLANES + c0
            acc = acc + wc_ref[kk:kk + 1, :] * shc[o % SUBLANES, a0:a0 + CONV_CHUNK]
        u = acc + bc_ref[...]
        mu = jnp.mean(u, axis=-1, keepdims=True)
        var = jnp.mean(jnp.square(u - mu), axis=-1, keepdims=True)
        y = (u - mu) * lax.rsqrt(var + EPS) * gl_ref[...] + bl_ref[...]
        yd_ref[0, c0:c0 + CONV_CHUNK] = (y * jax.nn.sigmoid(y)).astype(BF16)


def _conv_call(scb, scch, cfu, ws, wc, bc, gl, bl, tc):
    b, s, g = scb.shape
    row = pl.BlockSpec((1, tc, g), lambda i, j: (i, j, 0))
    out = jax.ShapeDtypeStruct((b, s, g), BF16)
    return pl.pallas_call(
        _conv_kernel,
        grid=(b, s // tc),
        in_specs=[row, row, row, _const_spec(ws.shape), _const_spec(wc.shape), _const_spec((1, g)),
                  _const_spec((1, g)), _const_spec((1, g))],
        out_specs=[row, row],
        out_shape=[out, out],
        scratch_shapes=[pltpu.VMEM((tc + SHORT_HALO, g), F32), pltpu.VMEM((tc + CONV_HALO, g), F32),
                        pltpu.VMEM((SUBLANES, tc + CONV_HALO, g), F32)],
        compiler_params=_cparams(("arbitrary", "arbitrary")),
        name="causal_convs",
    )(scb, scch, cfu, ws, wc, bc, gl, bl)


def _softplus(z):
    return jnp.maximum(z, 0.0) + jnp.log1p(jnp.exp(-jnp.abs(z)))


def _suffix_sum(l, upper):
    hi, lo = _split_bf16(l)
    return (jnp.dot(hi, upper, preferred_element_type=F32) + jnp.dot(lo, upper, preferred_element_type=F32))


def _sb_kernel(q_ref, k_ref, v_ref, o_ref, r_ref, acc_ref, z_ref, c_ref):
    t = SB_BLOCK
    w = 2 * t
    n_sub = q_ref.shape[1] // t
    step = pl.program_id(1)
    ur = lax.broadcasted_iota(jnp.int32, (w, w), 0)
    uc = lax.broadcasted_iota(jnp.int32, (w, w), 1)
    upper_w = (ur >= uc).astype(BF16)
    upper = upper_w[:t, :t]
    low_head = lax.broadcasted_iota(jnp.int32, (t, t), 1) < HEAD_DIM
    nt = (((1,), (1,)), ((), ()))

    def qmask(u, h):
        pair = q_ref[0, u * t:(u + 1) * t, (h // 2) * LANES:(h // 2 + 1) * LANES]
        keep = low_head if h % 2 == 0 else jnp.logical_not(low_head)
        return jnp.where(keep, pair, jnp.zeros_like(pair))

    def kv(ref, start, n, h):
        return ref[0, pl.ds(start, n), (h // 2) * LANES:(h // 2 + 1) * LANES]

    chains = [(u, h) for u in range(n_sub) for h in range(N_HEADS)]
    first_tile, wstarts, valids = [], [], []
    for u in range(n_sub):
        i = step * n_sub + u
        first = jnp.maximum(i - 1, 0)
        wstart = pl.multiple_of(first * t, t)
        kpos = wstart + lax.broadcasted_iota(jnp.int32, (t, w), 1)
        qpos = i * t + lax.broadcasted_iota(jnp.int32, (t, w), 0)
        first_tile.append(first)
        wstarts.append(wstart)
        valids.append(kpos < qpos)
    for u, h in chains:
        z_ref[u, h] = lax.dot_general(qmask(u, h), kv(k_ref, wstarts[u], w, h), nt, preferred_element_type=F32)
    for u, h in chains:
        l = jnp.where(valids[u], -_softplus(z_ref[u, h]), 0.0)
        c_ref[u, h] = _suffix_sum(l, upper_w)
    rmax0 = [jnp.float32(-jnp.inf)] * n_sub
    for u, h in chains:
        c = c_ref[u, h]
        a = jnp.where(valids[u], jnp.exp(z_ref[u, h] + c), 0.0)
        acc_ref[u, h] = jnp.dot(a.astype(BF16), kv(v_ref, wstarts[u], w, h), preferred_element_type=F32)
        r = jnp.broadcast_to(c[:, 0:1], (t, t))
        r_ref[u, h] = r
        rmax0[u] = jnp.maximum(rmax0[u], jnp.max(r))

    def cond(carry):
        j, rmax = carry
        return jnp.logical_and(j >= 0, rmax > SB_EXIT)

    for u in range(n_sub):
        def body(carry, u=u):
            j, _ = carry
            start = pl.multiple_of(j * t, t)
            rmax = jnp.float32(-jnp.inf)
            for h in range(N_HEADS):
                z = lax.dot_general(qmask(u, h), kv(k_ref, start, t, h), nt, preferred_element_type=F32)
                r = r_ref[u, h]
                c = _suffix_sum(-_softplus(z), upper)
                a = jnp.exp(z + c + r)
                acc_ref[u, h] = acc_ref[u, h] + jnp.dot(a.astype(BF16), kv(v_ref, start, t, h),
                                                        preferred_element_type=F32)
                r_new = r + jnp.broadcast_to(c[:, 0:1], (t, t))
                r_ref[u, h] = r_new
                rmax = jnp.maximum(rmax, jnp.max(r_new))
            return j - 1, rmax

        lax.while_loop(cond, body, (first_tile[u] - 1, rmax0[u]))

    for u in range(n_sub):
        for p in range(N_HEADS // 2):
            o_ref[0, u * t:(u + 1) * t, p * LANES:(p + 1) * LANES] = jnp.where(
                low_head, acc_ref[u, 2 * p], acc_ref[u, 2 * p + 1]).astype(BF16)


SB_STEP = 256


def _sb_call(q, k, v):
    b, s, g = q.shape
    t = SB_BLOCK
    n_sub = SB_STEP // t
    seq = pl.BlockSpec((1, s, g), lambda i, j: (i, 0, 0))
    tile = pl.BlockSpec((1, SB_STEP, g), lambda i, j: (i, j, 0))
    return pl.pallas_call(
        _sb_kernel,
        grid=(b, s // SB_STEP),
        in_specs=[tile, seq, seq],
        out_specs=tile,
        out_shape=jax.ShapeDtypeStruct((b, s, g), BF16),
        scratch_shapes=[pltpu.VMEM((n_sub, N_HEADS, t, t), F32), pltpu.VMEM((n_sub, N_HEADS, t, t), F32),
                        pltpu.VMEM((n_sub, N_HEADS, t, 2 * t), F32), pltpu.VMEM((n_sub, N_HEADS, t, 2 * t), F32)],
        compiler_params=_cparams(("arbitrary", "arbitrary")),
        name="stick_breaking",
    )(q, k, v)


def _moba_kernel(q_ref, kaug_ref, vaug_ref, km_ref, pq_ref, po_ref, o_ref, qa_ref, m_ref, acc_ref, sa_ref, sb_ref):
    t = MOBA_BLOCK
    tk = MOBA_KV_TILE
    i = pl.program_id(1)
    lane = lax.broadcasted_iota(jnp.int32, (t, AUG), 1)
    in_gate = jnp.logical_and(lane >= GATE_LANE0, lane < GATE_LANE0 + GATE_WIDTH)
    nt = (((1,), (1,)), ((), ()))

    def kblk(ref, start, n, h):
        return ref[0, pl.ds(start, n), h * AUG:(h + 1) * AUG]

    qf = q_ref[0]
    qb = (qf * (1.0 / math.sqrt(HEAD_DIM))).astype(BF16)

    gr = lax.broadcasted_iota(jnp.int32, (N_HEADS * GATE_WIDTH, GROUP), 0)
    gc = lax.broadcasted_iota(jnp.int32, (N_HEADS * GATE_WIDTH, GROUP), 1)
    km = km_ref[0]
    same_head = jnp.right_shift(gr, _log2(GATE_WIDTH)) == jnp.right_shift(gc, _log2(HEAD_DIM))
    km4 = jnp.where(same_head, jnp.concatenate([km] * N_HEADS, axis=0), 0.0)
    k_hi, k_lo = _split_bf16(km4)
    q_hi, q_lo = _split_bf16(qf)
    gate_t = (lax.dot_general(k_hi, q_hi, nt, preferred_element_type=F32)
              + lax.dot_general(k_hi, q_lo, nt, preferred_element_type=F32)
              + lax.dot_general(k_lo, q_hi, nt, preferred_element_type=F32))
    blk_id = lax.broadcasted_iota(jnp.int32, (GATE_WIDTH, t), 0)
    sel_rows = []
    for h in range(N_HEADS):
        gsc = jnp.where(blk_id < i, gate_t[h * GATE_WIDTH:(h + 1) * GATE_WIDTH], -jnp.inf)
        sel = blk_id == i
        for _ in range(MOBA_TOPK):
            mx = jnp.max(gsc, axis=0, keepdims=True)
            first = jnp.min(jnp.where(gsc == mx, blk_id, GATE_WIDTH), axis=0, keepdims=True)
            pick = jnp.logical_and(blk_id == first, mx > -jnp.inf)
            sel = jnp.logical_or(sel, pick)
            gsc = jnp.where(pick, -jnp.inf, gsc)
        sel_rows.append(jnp.where(sel, 1.0, 0.0))
    sel_q = jnp.transpose(jnp.concatenate(sel_rows, axis=0))
    qsel = jnp.concatenate([qb, sel_q.astype(BF16)], axis=1)

    for h in range(N_HEADS):
        qa = jnp.dot(qsel, pq_ref[h], preferred_element_type=F32)
        qa_ref[0, h] = jnp.where(in_gate, jnp.where(qa > 0.5, 0.0, -MASK_BIG), qa).astype(BF16)
        qa_ref[1, h] = jnp.where(in_gate, -MASK_BIG, qa).astype(BF16)
        m_ref[h] = jnp.full((t, AUG), -jnp.inf, F32)
        acc_ref[h] = jnp.zeros((t, AUG), F32)

    def scores(j, dst_ref, reject):
        start = pl.multiple_of(j * tk, tk)
        for h in range(N_HEADS):
            dst_ref[h] = lax.dot_general(qa_ref[reject, h], kblk(kaug_ref, start, tk, h), nt,
                                         preferred_element_type=F32)

    def softmax_pv(j, src_ref, mask=None):
        start = pl.multiple_of(j * tk, tk)
        for h in range(N_HEADS):
            s = src_ref[h]
            if mask is not None:
                s = jnp.where(mask, s, -jnp.inf)
            m_old = m_ref[h]
            m_new = jnp.maximum(m_old, jnp.max(s, axis=-1, keepdims=True))
            alpha = jnp.exp(m_old - m_new)
            p = jnp.exp(s - jnp.concatenate([m_new] * (tk // AUG), axis=1))
            acc_ref[h] = acc_ref[h] * alpha + jnp.dot(p.astype(BF16), kblk(vaug_ref, start, tk, h),
                                                      preferred_element_type=F32)
            m_ref[h] = m_new

    n_past = jnp.right_shift(i * t, _log2(tk))
    last = jnp.maximum(n_past - 1, 0)
    reject = lambda j: (j >= n_past).astype(jnp.int32)
    drow = lax.broadcasted_iota(jnp.int32, (t, tk), 0)
    dcol = lax.broadcasted_iota(jnp.int32, (t, tk), 1)
    diag_mask = n_past * tk + dcol <= i * t + drow
    scores(n_past, sa_ref, 0)
    scores(0, sb_ref, reject(0))
    softmax_pv(n_past, sa_ref, diag_mask)

    def body(jj, carry):
        a = 2 * jj
        scores(jnp.minimum(a + 1, last), sa_ref, reject(a + 1))
        softmax_pv(a, sb_ref)
        scores(jnp.minimum(a + 2, last), sb_ref, reject(a + 2))
        softmax_pv(jnp.minimum(a + 1, last), sa_ref)
        return carry

    lax.fori_loop(0, jnp.right_shift(n_past + 1, 1), body, 0)

    out = jnp.zeros((t, GROUP), F32)
    for h in range(N_HEADS):
        acc = acc_ref[h]
        denom = pltpu.roll(acc, HEAD_DIM, 1)
        y = jnp.where(lane < HEAD_DIM, acc / denom, 0.0).astype(BF16)
        out = out + jnp.dot(y, po_ref[h], preferred_element_type=F32)
    o_ref[0] = out.astype(BF16)


def _moba_call(q, kaug, vaug, km, pq, po):
    b, s, g = q.shape
    t = MOBA_BLOCK
    seq = pl.BlockSpec((1, s, N_HEADS * AUG), lambda i, j: (i, 0, 0))
    tile = pl.BlockSpec((1, t, g), lambda i, j: (i, j, 0))
    return pl.pallas_call(
        _moba_kernel,
        grid=(b, s // t),
        in_specs=[tile, seq, seq, pl.BlockSpec((1, GATE_WIDTH, g), lambda i, j: (i, 0, 0)),
                  _const_spec(pq.shape), _const_spec(po.shape)],
        out_specs=tile,
        out_shape=jax.ShapeDtypeStruct((b, s, g), BF16),
        scratch_shapes=[pltpu.VMEM((2, N_HEADS, t, AUG), BF16), pltpu.VMEM((N_HEADS, t, AUG), F32),
                        pltpu.VMEM((N_HEADS, t, AUG), F32), pltpu.VMEM((N_HEADS, t, MOBA_KV_TILE), F32),
                        pltpu.VMEM((N_HEADS, t, MOBA_KV_TILE), F32)],
        compiler_params=_cparams(("arbitrary", "arbitrary")),
        name="moba",
    )(q, kaug, vaug, km, pq, po)


MLP_CHUNK = 512


def _outmlp_kernel(x_ref, ya_ref, yb_ref, yc_ref, yd_ref, g1_ref, sh_ref, sc_ref, g2_ref, gn_ref, wo_ref,
                   w1_ref, w2_ref, o_ref):
    g = GROUP
    mix = jnp.dot(ya_ref[0], wo_ref[0:g], preferred_element_type=F32)
    mix = mix + jnp.dot(yb_ref[0], wo_ref[g:2 * g], preferred_element_type=F32)
    mix = mix + jnp.dot(yc_ref[0], wo_ref[2 * g:3 * g], preferred_element_type=F32)
    mix = mix + jnp.dot(yd_ref[0], wo_ref[3 * g:4 * g], preferred_element_type=F32)
    x1 = x_ref[0] + g1_ref[0] * mix
    ms = jnp.mean(x1 * x1, axis=-1, keepdims=True)
    h = x1 * lax.rsqrt(ms + EPS) * gn_ref[...]
    hb = (h * (1.0 + sc_ref[0]) + sh_ref[0]).astype(BF16)
    acc = jnp.zeros(x1.shape, F32)
    for f0 in range(0, D_FF, MLP_CHUNK):
        a = jnp.maximum(jnp.dot(hb, w1_ref[:, f0:f0 + MLP_CHUNK], preferred_element_type=F32), 0.0)
        acc = acc + jnp.dot((a * a).astype(BF16), w2_ref[f0:f0 + MLP_CHUNK, :], preferred_element_type=F32)
    o_ref[0] = x1 + g2_ref[0] * acc


def _outmlp_call(x, ya, yb, yc, yd, gate1, shift2, scale2, gate2, gn2, wo, w1, w2, tm):
    b, s, d = x.shape
    row = lambda w: pl.BlockSpec((1, tm, w), lambda i, j: (i, j, 0))
    mod = pl.BlockSpec((1, 1, d), lambda i, j: (i, 0, 0))
    return pl.pallas_call(
        _outmlp_kernel,
        grid=(b, s // tm),
        in_specs=[row(d), row(GROUP), row(GROUP), row(GROUP), row(GROUP), mod, mod, mod, mod,
                  _const_spec((1, d)), _const_spec(wo.shape), _const_spec(w1.shape), _const_spec(w2.shape)],
        out_specs=row(d),
        out_shape=jax.ShapeDtypeStruct((b, s, d), F32),
        compiler_params=_cparams(("arbitrary", "arbitrary")),
        name="out_mlp",
    )(x, ya, yb, yc, yd, gate1, shift2, scale2, gate2, gn2, wo, w1, w2)


def _placement_matrices():
    src = jnp.arange(GROUP)
    dst = (src // HEAD_DIM) * AUG + src % HEAD_DIM
    pk = jnp.zeros((GROUP, N_HEADS * AUG), BF16).at[src, dst].set(1.0)
    head = jnp.arange(N_HEADS)[:, None, None]
    r = jnp.arange(GROUP)[None, :, None]
    c = jnp.arange(AUG)[None, None, :]
    pq = ((r // HEAD_DIM == head) & (c == r % HEAD_DIM)).astype(BF16)
    po = jnp.transpose(pq, (0, 2, 1))
    rs = jnp.arange(N_HEADS * GATE_WIDTH)[None, :, None]
    ps = ((rs // GATE_WIDTH == head) & (c == GATE_LANE0 + rs % GATE_WIDTH)).astype(BF16)
    pq = jnp.concatenate([pq, ps], axis=1)
    eh = (src[:, None] // HEAD_DIM == src[None, :] // HEAD_DIM).astype(BF16)
    return pk, pq, po, eh


def _pick_tile(s, pref):
    t = pref
    while s % t:
        t //= 2
    return t


def kernel(x, c, positions, w_ada, b_ada, g_norm1, w_in, w_sconv, w_cconv, b_cconv, g_cln, b_cln, g_q, g_k,
           w_out, g_norm2, w_mlp1, w_mlp2):
    b, s, d = x.shape
    depth = w_ada.shape[0]
    nb = s // MOBA_BLOCK
    assert d == D_MODEL and s % MOBA_KV_TILE == 0 and nb <= GATE_WIDTH
    tm = _pick_tile(s, 512)
    pk, pq, po, eh = _placement_matrices()
    mod = _ada_call(c, w_ada, b_ada).reshape(depth, b, 6, 1, d)
    cos_t, sin_t = _rope_call(positions)
    tile4 = lambda v: jnp.tile(v, N_HEADS).reshape(1, GROUP)
    for l in range(depth):
        shift1, scale1, gate1, shift2, scale2, gate2 = (mod[l, :, i] for i in range(6))
        scb, scch, sbq, sbk, sbv, mbq, kaug, vaug, kmean, cfu = _inproj_call(
            x, shift1, scale1, g_norm1[l].reshape(1, d), w_in[l].astype(BF16), cos_t, sin_t,
            tile4(g_q[l]), tile4(g_k[l]), pk, eh, tm)
        ya, yd = _conv_call(scb, scch, cfu, w_sconv[l], w_cconv[l], b_cconv[l].reshape(1, GROUP),
                            g_cln[l].reshape(1, GROUP), b_cln[l].reshape(1, GROUP), tm)
        yb = _sb_call(sbq, sbk, sbv)
        km = jnp.pad(kmean.reshape(b, nb, GROUP), ((0, 0), (0, GATE_WIDTH - nb), (0, 0)))
        yc = _moba_call(mbq, kaug, vaug, km, pq, po)
        x = _outmlp_call(x, ya, yb, yc, yd, gate1, shift2, scale2, gate2, g_norm2[l].reshape(1, d),
                         w_out[l].astype(BF16), w_mlp1[l].astype(BF16), w_mlp2[l].astype(BF16), tm)
    return x
```

```python
import functools
import math

import jax
import jax.numpy as jnp
from jax import lax
from jax.experimental import pallas as pl
from jax.experimental.pallas import tpu as pltpu

F32 = jnp.float32
BF16 = jnp.bfloat16
HIGHEST = lax.Precision.HIGHEST

D_MODEL = 1024
GROUP = 256
HEAD_DIM = 64
N_HEADS = GROUP // HEAD_DIM
IN_WIDTH = 11 * GROUP
D_FF = 4 * D_MODEL
SHORT_K = 3
CONF_K = 31
SB_BLOCK = 128
MOBA_BLOCK = 256
MOBA_TOPK = 3
MOBA_KV_TILE = 512
ROPE_THETA = 10000.0
EPS = 1e-6
LANES = 128
SUBLANES = 8
AUG = 128
GATE_LANE0 = HEAD_DIM
GATE_WIDTH = 32
MASK_BIG = 2.0 ** 100
SB_EXIT = -90.0
VMEM_LIMIT = 56 * 1024 * 1024


def _log2(n):
    assert n & (n - 1) == 0
    return n.bit_length() - 1


def _cparams(sem):
    return pltpu.CompilerParams(dimension_semantics=sem, vmem_limit_bytes=VMEM_LIMIT)


def _split_bf16(a):
    hi = a.astype(BF16)
    return hi, (a - hi.astype(F32)).astype(BF16)


def _const_spec(shape):
    n = len(shape)
    return pl.BlockSpec(shape, lambda *_: (0,) * n, pipeline_mode=pl.Buffered(1))


def _ada_kernel(c_ref, w_ref, b_ref, o_ref):
    c = c_ref[...]
    ca = c * jax.nn.sigmoid(c)
    o_ref[0] = jnp.dot(ca, w_ref[0], precision=HIGHEST, preferred_element_type=F32) + b_ref[0]


def _ada_call(c, w_ada, b_ada):
    depth, d, n = w_ada.shape
    b = c.shape[0]
    tn = 1536
    return pl.pallas_call(
        _ada_kernel,
        grid=(depth, n // tn),
        in_specs=[pl.BlockSpec((b, d), lambda l, j: (0, 0)),
                  pl.BlockSpec((1, d, tn), lambda l, j: (l, 0, j)),
                  pl.BlockSpec((1, 1, tn), lambda l, j: (l, 0, j))],
        out_specs=pl.BlockSpec((1, b, tn), lambda l, j: (l, 0, j)),
        out_shape=jax.ShapeDtypeStruct((depth, b, n), F32),
        compiler_params=_cparams(("arbitrary", "arbitrary")),
        name="ada_mod",
    )(c, w_ada, b_ada.reshape(depth, 1, n))


def _rope_kernel(pos_ref, cos_ref, sin_ref):
    pos = pos_ref[0].astype(F32)
    lane = lax.broadcasted_iota(jnp.int32, (1, LANES), 1)
    half = HEAD_DIM // 2
    idx = (lane % half).astype(F32)
    inv_freq = jnp.exp((-math.log(ROPE_THETA) * idx) / half)
    ang = pos * inv_freq
    sign = jnp.where((lane % HEAD_DIM) < half, -1.0, 1.0)
    cos_ref[0] = jnp.cos(ang)
    sin_ref[0] = jnp.sin(ang) * sign


def _rope_call(positions):
    b, s = positions.shape
    ts = 1024
    spec = pl.BlockSpec((1, ts, LANES), lambda i, j: (i, j, 0))
    return pl.pallas_call(
        _rope_kernel,
        grid=(b, s // ts),
        in_specs=[pl.BlockSpec((1, ts, 1), lambda i, j: (i, j, 0))],
        out_specs=[spec, spec],
        out_shape=[jax.ShapeDtypeStruct((b, s, LANES), F32)] * 2,
        compiler_params=_cparams(("arbitrary", "arbitrary")),
        name="rope_table",
    )(positions.reshape(b, s, 1))


def _inproj_kernel(x_ref, sh_ref, sc_ref, g1_ref, w_ref, cos_ref, sin_ref, gq_ref, gk_ref, pk_ref,
                   eh_ref, pq_ref, scb_ref, scch_ref, sbq_ref, sbk_ref, sbv_ref, qaug_ref, kaug_ref, vaug_ref,
                   cfu_ref, km_scr):
    tm = x_ref.shape[1]
    g = GROUP
    x = x_ref[0]
    ms = jnp.mean(x * x, axis=-1, keepdims=True)
    h = x * lax.rsqrt(ms + EPS) * g1_ref[...]
    h = h * (1.0 + sc_ref[0]) + sh_ref[0]
    hb = h.astype(BF16)

    def proj(c0, n):
        return jnp.dot(hb, w_ref[:, c0:c0 + n], preferred_element_type=F32)

    p = proj(0, 3 * g)
    scb_ref[0] = p[:, :g]
    scch_ref[0] = p[:, g:2 * g] * p[:, 2 * g:]

    p = proj(3 * g, 3 * g)
    sbq_ref[0] = (p[:, :g] * (1.0 / math.sqrt(HEAD_DIM))).astype(BF16)
    sbk_ref[0] = p[:, g:2 * g].astype(BF16)
    sbv_ref[0] = p[:, 2 * g:].astype(BF16)

    p = proj(6 * g, 3 * g)
    cos_t = cos_ref[0]
    sin_t = sin_ref[0]
    lane = lax.broadcasted_iota(jnp.int32, (tm, LANES), 1)
    first_half = (lane % HEAD_DIM) < (HEAD_DIM // 2)
    half = HEAD_DIM // 2

    def norm_rope(t, gain):
        sq_hi, sq_lo = _split_bf16(t * t)
        ss = (jnp.dot(sq_hi, eh_ref[...], preferred_element_type=F32)
              + jnp.dot(sq_lo, eh_ref[...], preferred_element_type=F32))
        y = t * lax.rsqrt(ss * (1.0 / HEAD_DIM) + EPS) * gain
        out = []
        for c0 in range(0, g, LANES):
            yh = y[:, c0:c0 + LANES]
            partner = jnp.where(first_half, pltpu.roll(yh, LANES - half, 1), pltpu.roll(yh, half, 1))
            out.append(yh * cos_t + partner * sin_t)
        return jnp.concatenate(out, axis=1)

    q = norm_rope(p[:, :g], gq_ref[...])
    k = norm_rope(p[:, g:2 * g], gk_ref[...])

    nbt = tm // MOBA_BLOCK
    jstep = pl.program_id(1)

    @pl.when(jstep == 0)
    def _():
        km_scr[...] = jnp.zeros(km_scr.shape, F32)

    for r in range(nbt):
        km_scr[pl.ds(jstep * nbt + r, 1), :] = jnp.mean(k[r * MOBA_BLOCK:(r + 1) * MOBA_BLOCK], axis=0,
                                                        keepdims=True)

    nt = (((1,), (1,)), ((), ()))
    gr = lax.broadcasted_iota(jnp.int32, (N_HEADS * GATE_WIDTH, g), 0)
    gc = lax.broadcasted_iota(jnp.int32, (N_HEADS * GATE_WIDTH, g), 1)
    same_head = jnp.right_shift(gr, _log2(GATE_WIDTH)) == jnp.right_shift(gc, _log2(HEAD_DIM))
    km4 = jnp.where(same_head, jnp.concatenate([km_scr[...]] * N_HEADS, axis=0), 0.0)
    k_hi, k_lo = _split_bf16(km4)
    q_hi, q_lo = _split_bf16(q)
    gate_t = (lax.dot_general(k_hi, q_hi, nt, preferred_element_type=F32)
              + lax.dot_general(k_hi, q_lo, nt, preferred_element_type=F32)
              + lax.dot_general(k_lo, q_hi, nt, preferred_element_type=F32))
    blk_id = lax.broadcasted_iota(jnp.int32, (GATE_WIDTH, tm), 0)
    own = jstep * nbt + jnp.right_shift(lax.broadcasted_iota(jnp.int32, (GATE_WIDTH, tm), 1), _log2(MOBA_BLOCK))
    sel_rows = []
    for hh in range(N_HEADS):
        gsc = jnp.where(blk_id < own, gate_t[hh * GATE_WIDTH:(hh + 1) * GATE_WIDTH], -jnp.inf)
        sel = blk_id == own
        for _ in range(MOBA_TOPK):
            mx = jnp.max(gsc, axis=0, keepdims=True)
            first = jnp.min(jnp.where(gsc == mx, blk_id, GATE_WIDTH), axis=0, keepdims=True)
            pick = jnp.logical_and(blk_id == first, mx > -jnp.inf)
            sel = jnp.logical_or(sel, pick)
            gsc = jnp.where(pick, -jnp.inf, gsc)
        sel_rows.append(jnp.where(sel, 1.0, 0.0))
    sel_q = jnp.transpose(jnp.concatenate(sel_rows, axis=0))

    lane_a = lax.broadcasted_iota(jnp.int32, (tm, N_HEADS * AUG), 1) % AUG
    in_gate = jnp.logical_and(lane_a >= GATE_LANE0, lane_a < GATE_LANE0 + GATE_WIDTH)
    qsel = jnp.concatenate([(q * (1.0 / math.sqrt(HEAD_DIM))).astype(BF16), sel_q.astype(BF16)], axis=1)
    qa = jnp.dot(qsel, pq_ref[...], preferred_element_type=F32)
    qaug_ref[0, 0] = jnp.where(in_gate, jnp.where(qa > 0.5, 0.0, -MASK_BIG), qa).astype(BF16)
    qaug_ref[1, 0] = jnp.where(in_gate, -MASK_BIG, qa).astype(BF16)

    row = lax.broadcasted_iota(jnp.int32, (tm, N_HEADS * AUG), 0)
    blk = jstep * nbt + jnp.right_shift(row, _log2(MOBA_BLOCK))
    kaug = jnp.dot(k.astype(BF16), pk_ref[...], preferred_element_type=F32)
    kaug_ref[0] = jnp.where(lane_a - GATE_LANE0 == blk, 1.0, kaug).astype(BF16)
    vaug = jnp.dot(p[:, 2 * g:].astype(BF16), pk_ref[...], preferred_element_type=F32)
    vaug_ref[0] = jnp.where(lane_a >= HEAD_DIM, 1.0, vaug).astype(BF16)

    p = proj(9 * g, 2 * g)
    cfu_ref[0] = p[:, :g] * jax.nn.sigmoid(p[:, g:])


def _inproj_call(x, shift, scale, g1, w_in, cos_t, sin_t, gq, gk, pk, eh, pq, tm):
    b, s, d = x.shape
    g = GROUP
    wa = N_HEADS * AUG
    row = lambda w: pl.BlockSpec((1, tm, w), lambda i, j: (i, j, 0))
    mod = pl.BlockSpec((1, 1, d), lambda i, j: (i, 0, 0))
    f32o = lambda w: jax.ShapeDtypeStruct((b, s, w), F32)
    bf16o = lambda w: jax.ShapeDtypeStruct((b, s, w), BF16)
    return pl.pallas_call(
        _inproj_kernel,
        grid=(b, s // tm),
        in_specs=[row(d), mod, mod, _const_spec((1, d)), _const_spec(w_in.shape), row(LANES), row(LANES),
                  _const_spec((1, g)), _const_spec((1, g)), _const_spec(pk.shape), _const_spec(eh.shape),
                  _const_spec(pq.shape)],
        out_specs=[row(g), row(g), row(g), row(g), row(g),
                   pl.BlockSpec((2, 1, tm, wa), lambda i, j: (0, i, j, 0)), row(wa), row(wa), row(g)],
        out_shape=[f32o(g), f32o(g), bf16o(g), bf16o(g), bf16o(g), jax.ShapeDtypeStruct((2, b, s, wa), BF16),
                   bf16o(wa), bf16o(wa), f32o(g)],
        scratch_shapes=[pltpu.VMEM((GATE_WIDTH, g), F32)],
        compiler_params=_cparams(("arbitrary", "arbitrary")),
        name="in_proj",
    )(x, shift, scale, g1, w_in, cos_t, sin_t, gq, gk, pk, eh, pq)


CONV_HALO = 32
SHORT_HALO = 8
CONV_CHUNK = 64


def _conv_kernel(scb_ref, scch_ref, cfu_ref, ws_ref, wc_ref, bc_ref, gl_ref, bl_ref, ya_ref, yd_ref,
                 exts, extc, shc):
    tc = scb_ref.shape[1]
    j = pl.program_id(1)

    @pl.when(j == 0)
    def _():
        exts[0:SHORT_HALO] = jnp.zeros((SHORT_HALO, GROUP), F32)
        extc[0:CONV_HALO] = jnp.zeros((CONV_HALO, GROUP), F32)

    @pl.when(j > 0)
    def _():
        exts[0:SHORT_HALO] = exts[tc:tc + SHORT_HALO]
        extc[0:CONV_HALO] = extc[tc:tc + CONV_HALO]

    exts[SHORT_HALO:SHORT_HALO + tc] = scch_ref[0]
    extc[CONV_HALO:CONV_HALO + tc] = cfu_ref[0]

    n_ext = tc + CONV_HALO
    for r in range(SUBLANES):
        n = n_ext - (SUBLANES if r else 0)
        shc[r, 0:n] = extc[r:r + n]

    for c0 in range(0, tc, CONV_CHUNK):
        acc = jnp.zeros((CONV_CHUNK, GROUP), F32)
        for kk in range(SHORT_K):
            o = SHORT_HALO - (SHORT_K - 1) + kk + c0
            acc = acc + ws_ref[kk:kk + 1, :] * exts[o:o + CONV_CHUNK]
        ya_ref[0, c0:c0 + CONV_CHUNK] = (scb_ref[0, c0:c0 + CONV_CHUNK] * acc).astype(BF16)

        acc = jnp.zeros((CONV_CHUNK, GROUP), F32)
        for kk in range(CONF_K):
            o = CONV_HALO - (CONF_K - 1) + kk
            a0 = o - o % SUBLANES + c0
            acc = acc + wc_ref[kk:kk + 1, :] * shc[o % SUBLANES, a0:a0 + CONV_CHUNK]
        u = acc + bc_ref[...]
        mu = jnp.mean(u, axis=-1, keepdims=True)
        var = jnp.mean(jnp.square(u - mu), axis=-1, keepdims=True)
        y = (u - mu) * lax.rsqrt(var + EPS) * gl_ref[...] + bl_ref[...]
        yd_ref[0, c0:c0 + CONV_CHUNK] = (y * jax.nn.sigmoid(y)).astype(BF16)


def _conv_call(scb, scch, cfu, ws, wc, bc, gl, bl, tc):
    b, s, g = scb.shape
    row = pl.BlockSpec((1, tc, g), lambda i, j: (i, j, 0))
    out = jax.ShapeDtypeStruct((b, s, g), BF16)
    return pl.pallas_call(
        _conv_kernel,
        grid=(b, s // tc),
        in_specs=[row, row, row, _const_spec(ws.shape), _const_spec(wc.shape), _const_spec((1, g)),
                  _const_spec((1, g)), _const_spec((1, g))],
        out_specs=[row, row],
        out_shape=[out, out],
        scratch_shapes=[pltpu.VMEM((tc + SHORT_HALO, g), F32), pltpu.VMEM((tc + CONV_HALO, g), F32),
                        pltpu.VMEM((SUBLANES, tc + CONV_HALO, g), F32)],
        compiler_params=_cparams(("arbitrary", "arbitrary")),
        name="causal_convs",
    )(scb, scch, cfu, ws, wc, bc, gl, bl)


def _softplus(z):
    return jnp.maximum(z, 0.0) + jnp.log1p(jnp.exp(-jnp.abs(z)))


def _suffix_sum(l, upper):
    hi, lo = _split_bf16(l)
    return (jnp.dot(hi, upper, preferred_element_type=F32) + jnp.dot(lo, upper, preferred_element_type=F32))


def _sb_kernel(q_ref, k_ref, v_ref, o_ref, r_ref, acc_ref, z_ref, c_ref):
    t = SB_BLOCK
    w = SB_WINDOW * t
    n_sub = q_ref.shape[1] // t
    step = pl.program_id(1)
    ur = lax.broadcasted_iota(jnp.int32, (w, w), 0)
    uc = lax.broadcasted_iota(jnp.int32, (w, w), 1)
    upper_w = (ur >= uc).astype(BF16)
    upper = upper_w[:t, :t]
    low_head = lax.broadcasted_iota(jnp.int32, (t, t), 1) < HEAD_DIM
    nt = (((1,), (1,)), ((), ()))

    def qmask(u, h):
        pair = q_ref[0, u * t:(u + 1) * t, (h // 2) * LANES:(h // 2 + 1) * LANES]
        keep = low_head if h % 2 == 0 else jnp.logical_not(low_head)
        return jnp.where(keep, pair, jnp.zeros_like(pair))

    def kv(ref, start, n, h):
        return ref[0, pl.ds(start, n), (h // 2) * LANES:(h // 2 + 1) * LANES]

    chains = [(u, h) for u in range(n_sub) for h in range(N_HEADS)]
    first_tile, wstarts, valids = [], [], []
    for u in range(n_sub):
        i = step * n_sub + u
        first = jnp.maximum(i - (SB_WINDOW - 1), 0)
        wstart = pl.multiple_of(first * t, t)
        kpos = wstart + lax.broadcasted_iota(jnp.int32, (t, w), 1)
        qpos = i * t + lax.broadcasted_iota(jnp.int32, (t, w), 0)
        first_tile.append(first)
        wstarts.append(wstart)
        valids.append(kpos < qpos)
    for u, h in chains:
        z_ref[u, h] = lax.dot_general(qmask(u, h), kv(k_ref, wstarts[u], w, h), nt, preferred_element_type=F32)
    for u, h in chains:
        l = jnp.where(valids[u], -_softplus(z_ref[u, h]), 0.0)
        c_ref[u, h] = _suffix_sum(l, upper_w)
    rmax0 = [jnp.float32(-jnp.inf)] * n_sub
    for u, h in chains:
        c = c_ref[u, h]
        a = jnp.where(valids[u], jnp.exp(z_ref[u, h] + c), 0.0)
        acc_ref[u, h] = jnp.dot(a.astype(BF16), kv(v_ref, wstarts[u], w, h), preferred_element_type=F32)
        r = jnp.broadcast_to(c[:, 0:1], (t, t))
        r_ref[u, h] = r
        rmax0[u] = jnp.maximum(rmax0[u], jnp.max(r))

    def cond(carry):
        j, rmax = carry
        return jnp.logical_and(j >= 0, rmax > SB_EXIT)

    for u in range(n_sub):
        def body(carry, u=u):
            j, _ = carry
            start = pl.multiple_of(j * t, t)
            rmax = jnp.float32(-jnp.inf)
            for h in range(N_HEADS):
                z_ref[u, h, :, 0:t] = lax.dot_general(qmask(u, h), kv(k_ref, start, t, h), nt,
                                                      preferred_element_type=F32)
            for h in range(N_HEADS):
                c_ref[u, h, :, 0:t] = _suffix_sum(-_softplus(z_ref[u, h, :, 0:t]), upper)
            for h in range(N_HEADS):
                r = r_ref[u, h]
                c = c_ref[u, h, :, 0:t]
                a = jnp.exp(z_ref[u, h, :, 0:t] + c + r)
                acc_ref[u, h] = acc_ref[u, h] + jnp.dot(a.astype(BF16), kv(v_ref, start, t, h),
                                                        preferred_element_type=F32)
                r_new = r + jnp.broadcast_to(c[:, 0:1], (t, t))
                r_ref[u, h] = r_new
                rmax = jnp.maximum(rmax, jnp.max(r_new))
            return j - 1, rmax

        lax.while_loop(cond, body, (first_tile[u] - 1, rmax0[u]))

    for u in range(n_sub):
        for p in range(N_HEADS // 2):
            o_ref[0, u * t:(u + 1) * t, p * LANES:(p + 1) * LANES] = jnp.where(
                low_head, acc_ref[u, 2 * p], acc_ref[u, 2 * p + 1]).astype(BF16)


SB_STEP = 256
SB_WINDOW = 3


def _sb_call(q, k, v):
    b, s, g = q.shape
    t = SB_BLOCK
    n_sub = SB_STEP // t
    seq = pl.BlockSpec((1, s, g), lambda i, j: (i, 0, 0))
    tile = pl.BlockSpec((1, SB_STEP, g), lambda i, j: (i, j, 0))
    return pl.pallas_call(
        _sb_kernel,
        grid=(b, s // SB_STEP),
        in_specs=[tile, seq, seq],
        out_specs=tile,
        out_shape=jax.ShapeDtypeStruct((b, s, g), BF16),
        scratch_shapes=[pltpu.VMEM((n_sub, N_HEADS, t, t), F32), pltpu.VMEM((n_sub, N_HEADS, t, t), F32),
                        pltpu.VMEM((n_sub, N_HEADS, t, SB_WINDOW * t), F32),
                        pltpu.VMEM((n_sub, N_HEADS, t, SB_WINDOW * t), F32)],
        compiler_params=_cparams(("arbitrary", "arbitrary")),
        name="stick_breaking",
    )(q, k, v)


def _moba_kernel(qa_ref, kaug_ref, vaug_ref, po_ref, o_ref, m_ref, acc_ref, sa_ref, sb_ref):
    t = MOBA_BLOCK
    tk = MOBA_KV_TILE
    i = pl.program_id(1)
    lane = lax.broadcasted_iota(jnp.int32, (t, AUG), 1)
    nt = (((1,), (1,)), ((), ()))

    def kblk(ref, start, n, h):
        return ref[0, pl.ds(start, n), h * AUG:(h + 1) * AUG]

    for h in range(N_HEADS):
        m_ref[h] = jnp.full((t, AUG), -jnp.inf, F32)
        acc_ref[h] = jnp.zeros((t, AUG), F32)

    def scores(j, dst_ref, reject):
        start = pl.multiple_of(j * tk, tk)
        for h in range(N_HEADS):
            dst_ref[h] = lax.dot_general(qa_ref[reject, 0, :, h * AUG:(h + 1) * AUG],
                                         kblk(kaug_ref, start, tk, h), nt, preferred_element_type=F32)

    def softmax_pv(j, src_ref, mask=None):
        start = pl.multiple_of(j * tk, tk)
        for h in range(N_HEADS):
            s = src_ref[h]
            if mask is not None:
                s = jnp.where(mask, s, -jnp.inf)
            m_old = m_ref[h]
            m_new = jnp.maximum(m_old, jnp.max(s, axis=-1, keepdims=True))
            alpha = jnp.exp(m_old - m_new)
            p = jnp.exp(s - jnp.concatenate([m_new] * (tk // AUG), axis=1))
            acc_ref[h] = acc_ref[h] * alpha + jnp.dot(p.astype(BF16), kblk(vaug_ref, start, tk, h),
                                                      preferred_element_type=F32)
            m_ref[h] = m_new

    n_past = jnp.right_shift(i * t, _log2(tk))
    last = jnp.maximum(n_past - 1, 0)
    reject = lambda j: (j >= n_past).astype(jnp.int32)
    drow = lax.broadcasted_iota(jnp.int32, (t, tk), 0)
    dcol = lax.broadcasted_iota(jnp.int32, (t, tk), 1)
    diag_mask = n_past * tk + dcol <= i * t + drow
    scores(n_past, sa_ref, 0)
    scores(0, sb_ref, reject(0))
    softmax_pv(n_past, sa_ref, diag_mask)

    def body(jj, carry):
        a = 2 * jj
        scores(jnp.minimum(a + 1, last), sa_ref, reject(a + 1))
        softmax_pv(a, sb_ref)
        scores(jnp.minimum(a + 2, last), sb_ref, reject(a + 2))
        softmax_pv(jnp.minimum(a + 1, last), sa_ref)
        return carry

    lax.fori_loop(0, jnp.right_shift(n_past + 1, 1), body, 0)

    out = jnp.zeros((t, GROUP), F32)
    for h in range(N_HEADS):
        acc = acc_ref[h]
        denom = pltpu.roll(acc, HEAD_DIM, 1)
        y = jnp.where(lane < HEAD_DIM, acc / denom, 0.0).astype(BF16)
        out = out + jnp.dot(y, po_ref[h], preferred_element_type=F32)
    o_ref[0] = out.astype(BF16)


def _moba_call(qaug, kaug, vaug, po):
    _, b, s, wa = qaug.shape
    g = GROUP
    t = MOBA_BLOCK
    seq = pl.BlockSpec((1, s, wa), lambda i, j: (i, 0, 0))
    tile = pl.BlockSpec((1, t, g), lambda i, j: (i, j, 0))
    return pl.pallas_call(
        _moba_kernel,
        grid=(b, s // t),
        in_specs=[pl.BlockSpec((2, 1, t, wa), lambda i, j: (0, i, j, 0)), seq, seq, _const_spec(po.shape)],
        out_specs=tile,
        out_shape=jax.ShapeDtypeStruct((b, s, g), BF16),
        scratch_shapes=[pltpu.VMEM((N_HEADS, t, AUG), F32), pltpu.VMEM((N_HEADS, t, AUG), F32),
                        pltpu.VMEM((N_HEADS, t, MOBA_KV_TILE), F32), pltpu.VMEM((N_HEADS, t, MOBA_KV_TILE), F32)],
        compiler_params=_cparams(("arbitrary", "arbitrary")),
        name="moba",
    )(qaug, kaug, vaug, po)


MLP_CHUNK = 512


def _outmlp_kernel(x_ref, ya_ref, yb_ref, yc_ref, yd_ref, g1_ref, sh_ref, sc_ref, g2_ref, gn_ref, wo_ref,
                   w1_ref, w2_ref, o_ref):
    g = GROUP
    mix = jnp.dot(ya_ref[0], wo_ref[0:g], preferred_element_type=F32)
    mix = mix + jnp.dot(yb_ref[0], wo_ref[g:2 * g], preferred_element_type=F32)
    mix = mix + jnp.dot(yc_ref[0], wo_ref[2 * g:3 * g], preferred_element_type=F32)
    mix = mix + jnp.dot(yd_ref[0], wo_ref[3 * g:4 * g], preferred_element_type=F32)
    x1 = x_ref[0] + g1_ref[0] * mix
    ms = jnp.mean(x1 * x1, axis=-1, keepdims=True)
    h = x1 * lax.rsqrt(ms + EPS) * gn_ref[...]
    hb = (h * (1.0 + sc_ref[0]) + sh_ref[0]).astype(BF16)
    acc = jnp.zeros(x1.shape, F32)
    for f0 in range(0, D_FF, MLP_CHUNK):
        a = jnp.maximum(jnp.dot(hb, w1_ref[:, f0:f0 + MLP_CHUNK], preferred_element_type=F32), 0.0)
        acc = acc + jnp.dot((a * a).astype(BF16), w2_ref[f0:f0 + MLP_CHUNK, :], preferred_element_type=F32)
    o_ref[0] = x1 + g2_ref[0] * acc


def _outmlp_call(x, ya, yb, yc, yd, gate1, shift2, scale2, gate2, gn2, wo, w1, w2, tm):
    b, s, d = x.shape
    row = lambda w: pl.BlockSpec((1, tm, w), lambda i, j: (i, j, 0))
    mod = pl.BlockSpec((1, 1, d), lambda i, j: (i, 0, 0))
    return pl.pallas_call(
        _outmlp_kernel,
        grid=(b, s // tm),
        in_specs=[row(d), row(GROUP), row(GROUP), row(GROUP), row(GROUP), mod, mod, mod, mod,
                  _const_spec((1, d)), _const_spec(wo.shape), _const_spec(w1.shape), _const_spec(w2.shape)],
        out_specs=row(d),
        out_shape=jax.ShapeDtypeStruct((b, s, d), F32),
        compiler_params=_cparams(("arbitrary", "arbitrary")),
        name="out_mlp",
    )(x, ya, yb, yc, yd, gate1, shift2, scale2, gate2, gn2, wo, w1, w2)


def _placement_matrices():
    src = jnp.arange(GROUP)
    dst = (src // HEAD_DIM) * AUG + src % HEAD_DIM
    pk = jnp.zeros((GROUP, N_HEADS * AUG), BF16).at[src, dst].set(1.0)
    head = jnp.arange(N_HEADS)[:, None, None]
    r = jnp.arange(GROUP)[None, :, None]
    c = jnp.arange(AUG)[None, None, :]
    po = ((r // HEAD_DIM == head) & (c == r % HEAD_DIM)).astype(BF16)
    po = jnp.transpose(po, (0, 2, 1))
    rs = jnp.arange(N_HEADS * GATE_WIDTH)
    dst_s = (rs // GATE_WIDTH) * AUG + GATE_LANE0 + rs % GATE_WIDTH
    pq = jnp.concatenate([pk, jnp.zeros((N_HEADS * GATE_WIDTH, N_HEADS * AUG), BF16).at[rs, dst_s].set(1.0)],
                         axis=0)
    eh = (src[:, None] // HEAD_DIM == src[None, :] // HEAD_DIM).astype(BF16)
    return pk, pq, po, eh


def _pick_tile(s, pref):
    t = pref
    while s % t:
        t //= 2
    return t


def kernel(x, c, positions, w_ada, b_ada, g_norm1, w_in, w_sconv, w_cconv, b_cconv, g_cln, b_cln, g_q, g_k,
           w_out, g_norm2, w_mlp1, w_mlp2):
    b, s, d = x.shape
    depth = w_ada.shape[0]
    nb = s // MOBA_BLOCK
    assert d == D_MODEL and s % MOBA_KV_TILE == 0 and nb <= GATE_WIDTH
    tm = _pick_tile(s, 512)
    pk, pq, po, eh = _placement_matrices()
    mod = _ada_call(c, w_ada, b_ada).reshape(depth, b, 6, 1, d)
    cos_t, sin_t = _rope_call(positions)
    tile4 = lambda v: jnp.tile(v, N_HEADS).reshape(1, GROUP)
    for l in range(depth):
        shift1, scale1, gate1, shift2, scale2, gate2 = (mod[l, :, i] for i in range(6))
        scb, scch, sbq, sbk, sbv, qaug, kaug, vaug, cfu = _inproj_call(
            x, shift1, scale1, g_norm1[l].reshape(1, d), w_in[l].astype(BF16), cos_t, sin_t,
            tile4(g_q[l]), tile4(g_k[l]), pk, eh, pq, tm)
        ya, yd = _conv_call(scb, scch, cfu, w_sconv[l], w_cconv[l], b_cconv[l].reshape(1, GROUP),
                            g_cln[l].reshape(1, GROUP), b_cln[l].reshape(1, GROUP), tm)
        yb = _sb_call(sbq, sbk, sbv)
        yc = _moba_call(qaug, kaug, vaug, po)
        x = _outmlp_call(x, ya, yb, yc, yd, gate1, shift2, scale2, gate2, g_norm2[l].reshape(1, d),
                         w_out[l].astype(BF16), w_mlp1[l].astype(BF16), w_mlp2[l].astype(BF16), tm)
    return x
```

```python
import functools
import math

import jax
import jax.numpy as jnp
from jax import lax
from jax.experimental import pallas as pl
from jax.experimental.pallas import tpu as pltpu

F32 = jnp.float32
BF16 = jnp.bfloat16
HIGHEST = lax.Precision.HIGHEST

D_MODEL = 1024
GROUP = 256
HEAD_DIM = 64
N_HEADS = GROUP // HEAD_DIM
IN_WIDTH = 11 * GROUP
D_FF = 4 * D_MODEL
SHORT_K = 3
CONF_K = 31
CONV_HALO = 32
SHORT_HALO = 8
CONV_CHUNK = 64
SB_BLOCK = 128
MOBA_BLOCK = 256
MOBA_TOPK = 3
MOBA_KV_TILE = 512
MOBA_Q_TILE = 512
ROPE_THETA = 10000.0
EPS = 1e-6
LANES = 128
SUBLANES = 8
AUG = 128
GATE_LANE0 = HEAD_DIM
GATE_WIDTH = 32
MASK_BIG = 2.0 ** 100
SB_EXIT = -90.0
VMEM_LIMIT = 56 * 1024 * 1024


def _log2(n):
    assert n & (n - 1) == 0
    return n.bit_length() - 1


def _cparams(sem):
    return pltpu.CompilerParams(dimension_semantics=sem, vmem_limit_bytes=VMEM_LIMIT)


def _split_bf16(a):
    hi = a.astype(BF16)
    return hi, (a - hi.astype(F32)).astype(BF16)


def _const_spec(shape):
    n = len(shape)
    return pl.BlockSpec(shape, lambda *_: (0,) * n, pipeline_mode=pl.Buffered(1))


def _ada_kernel(c_ref, w_ref, b_ref, o_ref):
    c = c_ref[...]
    ca = c * jax.nn.sigmoid(c)
    o_ref[0] = jnp.dot(ca, w_ref[0], precision=HIGHEST, preferred_element_type=F32) + b_ref[0]


def _ada_call(c, w_ada, b_ada):
    depth, d, n = w_ada.shape
    b = c.shape[0]
    tn = 1536
    return pl.pallas_call(
        _ada_kernel,
        grid=(depth, n // tn),
        in_specs=[pl.BlockSpec((b, d), lambda l, j: (0, 0)),
                  pl.BlockSpec((1, d, tn), lambda l, j: (l, 0, j)),
                  pl.BlockSpec((1, 1, tn), lambda l, j: (l, 0, j))],
        out_specs=pl.BlockSpec((1, b, tn), lambda l, j: (l, 0, j)),
        out_shape=jax.ShapeDtypeStruct((depth, b, n), F32),
        compiler_params=_cparams(("arbitrary", "arbitrary")),
        name="ada_mod",
    )(c, w_ada, b_ada.reshape(depth, 1, n))


def _rope_kernel(pos_ref, cos_ref, sin_ref):
    pos = pos_ref[0].astype(F32)
    lane = lax.broadcasted_iota(jnp.int32, (1, LANES), 1)
    half = HEAD_DIM // 2
    idx = (lane % half).astype(F32)
    inv_freq = jnp.exp((-math.log(ROPE_THETA) * idx) / half)
    ang = pos * inv_freq
    sign = jnp.where((lane % HEAD_DIM) < half, -1.0, 1.0)
    cos_ref[0] = jnp.cos(ang)
    sin_ref[0] = jnp.sin(ang) * sign


def _rope_call(positions):
    b, s = positions.shape
    ts = 1024
    spec = pl.BlockSpec((1, ts, LANES), lambda i, j: (i, j, 0))
    return pl.pallas_call(
        _rope_kernel,
        grid=(b, s // ts),
        in_specs=[pl.BlockSpec((1, ts, 1), lambda i, j: (i, j, 0))],
        out_specs=[spec, spec],
        out_shape=[jax.ShapeDtypeStruct((b, s, LANES), F32)] * 2,
        compiler_params=_cparams(("arbitrary", "arbitrary")),
        name="rope_table",
    )(positions.reshape(b, s, 1))


def _inproj_kernel(x_ref, sh_ref, sc_ref, g1_ref, w_ref, cos_ref, sin_ref, gq_ref, gk_ref, pk_ref,
                   eh_ref, pq_ref, ws_ref, wc_ref, bc_ref, gl_ref, bl_ref,
                   ya_ref, yd_ref, sbq_ref, sbk_ref, sbv_ref, qaug_ref, kaug_ref, vaug_ref,
                   km_scr, scb_scr, exts, extc, shc):
    tm = x_ref.shape[1]
    g = GROUP
    nbt = tm // MOBA_BLOCK
    jstep = pl.program_id(1)

    @pl.when(jstep == 0)
    def _():
        km_scr[...] = jnp.zeros(km_scr.shape, F32)
        exts[0:SHORT_HALO] = jnp.zeros((SHORT_HALO, g), F32)
        extc[0:CONV_HALO] = jnp.zeros((CONV_HALO, g), F32)

    @pl.when(jstep > 0)
    def _():
        exts[0:SHORT_HALO] = exts[tm:tm + SHORT_HALO]
        extc[0:CONV_HALO] = extc[tm:tm + CONV_HALO]

    x = x_ref[0]
    ms = jnp.mean(x * x, axis=-1, keepdims=True)
    h = x * lax.rsqrt(ms + EPS) * g1_ref[...]
    h = h * (1.0 + sc_ref[0]) + sh_ref[0]
    hb = h.astype(BF16)

    def proj(c0, n):
        return jnp.dot(hb, w_ref[:, c0:c0 + n], preferred_element_type=F32)

    p = proj(0, 3 * g)
    scb_scr[...] = p[:, :g]
    exts[SHORT_HALO:SHORT_HALO + tm] = p[:, g:2 * g] * p[:, 2 * g:]

    p = proj(9 * g, 2 * g)
    extc[CONV_HALO:CONV_HALO + tm] = p[:, :g] * jax.nn.sigmoid(p[:, g:])

    n_ext = tm + CONV_HALO
    for r in range(SUBLANES):
        n = n_ext - (SUBLANES if r else 0)
        shc[r, 0:n] = extc[r:r + n]

    def conv_chunks(first, stop):
        for c0 in range(first * CONV_CHUNK, stop * CONV_CHUNK, CONV_CHUNK):
            acc = jnp.zeros((CONV_CHUNK, g), F32)
            for kk in range(SHORT_K):
                o = SHORT_HALO - (SHORT_K - 1) + kk + c0
                acc = acc + ws_ref[kk:kk + 1, :] * exts[o:o + CONV_CHUNK]
            ya_ref[0, c0:c0 + CONV_CHUNK] = (scb_scr[c0:c0 + CONV_CHUNK] * acc).astype(BF16)

            acc = jnp.zeros((CONV_CHUNK, g), F32)
            for kk in range(CONF_K):
                o = CONV_HALO - (CONF_K - 1) + kk
                a0 = o - o % SUBLANES + c0
                acc = acc + wc_ref[kk:kk + 1, :] * shc[o % SUBLANES, a0:a0 + CONV_CHUNK]
            u = acc + bc_ref[...]
            mu = jnp.mean(u, axis=-1, keepdims=True)
            var = jnp.mean(jnp.square(u - mu), axis=-1, keepdims=True)
            y = (u - mu) * lax.rsqrt(var + EPS) * gl_ref[...] + bl_ref[...]
            yd_ref[0, c0:c0 + CONV_CHUNK] = (y * jax.nn.sigmoid(y)).astype(BF16)

    n_chunks = tm // CONV_CHUNK

    p = proj(3 * g, 3 * g)
    sbq_ref[0] = (p[:, :g] * (1.0 / math.sqrt(HEAD_DIM))).astype(BF16)
    sbk_ref[0] = p[:, g:2 * g].astype(BF16)
    sbv_ref[0] = p[:, 2 * g:].astype(BF16)
    conv_chunks(0, n_chunks // 4)

    p = proj(6 * g, 3 * g)
    cos_t = cos_ref[0]
    sin_t = sin_ref[0]
    lane = lax.broadcasted_iota(jnp.int32, (tm, LANES), 1)
    first_half = (lane % HEAD_DIM) < (HEAD_DIM // 2)
    half = HEAD_DIM // 2

    def norm_rope(t, gain):
        sq_hi, sq_lo = _split_bf16(t * t)
        ss = (jnp.dot(sq_hi, eh_ref[...], preferred_element_type=F32)
              + jnp.dot(sq_lo, eh_ref[...], preferred_element_type=F32))
        y = t * lax.rsqrt(ss * (1.0 / HEAD_DIM) + EPS) * gain
        out = []
        for c0 in range(0, g, LANES):
            yh = y[:, c0:c0 + LANES]
            partner = jnp.where(first_half, pltpu.roll(yh, LANES - half, 1), pltpu.roll(yh, half, 1))
            out.append(yh * cos_t + partner * sin_t)
        return jnp.concatenate(out, axis=1)

    q = norm_rope(p[:, :g], gq_ref[...])
    k = norm_rope(p[:, g:2 * g], gk_ref[...])
    conv_chunks(n_chunks // 4, n_chunks // 2)

    for r in range(nbt):
        km_scr[pl.ds(jstep * nbt + r, 1), :] = jnp.mean(k[r * MOBA_BLOCK:(r + 1) * MOBA_BLOCK], axis=0,
                                                        keepdims=True)

    nt = (((1,), (1,)), ((), ()))
    gr = lax.broadcasted_iota(jnp.int32, (N_HEADS * GATE_WIDTH, g), 0)
    gc = lax.broadcasted_iota(jnp.int32, (N_HEADS * GATE_WIDTH, g), 1)
    same_head = jnp.right_shift(gr, _log2(GATE_WIDTH)) == jnp.right_shift(gc, _log2(HEAD_DIM))
    km4 = jnp.where(same_head, jnp.concatenate([km_scr[...]] * N_HEADS, axis=0), 0.0)
    k_hi, k_lo = _split_bf16(km4)
    q_hi, q_lo = _split_bf16(q)
    gate_t = (lax.dot_general(k_hi, q_hi, nt, preferred_element_type=F32)
              + lax.dot_general(k_hi, q_lo, nt, preferred_element_type=F32)
              + lax.dot_general(k_lo, q_hi, nt, preferred_element_type=F32))
    blk_id = lax.broadcasted_iota(jnp.int32, (GATE_WIDTH, tm), 0)
    own = jstep * nbt + jnp.right_shift(lax.broadcasted_iota(jnp.int32, (GATE_WIDTH, tm), 1), _log2(MOBA_BLOCK))
    sel_rows = []
    for hh in range(N_HEADS):
        gsc = jnp.where(blk_id < own, gate_t[hh * GATE_WIDTH:(hh + 1) * GATE_WIDTH], -jnp.inf)
        sel = blk_id == own
        for _ in range(MOBA_TOPK):
            mx = jnp.max(gsc, axis=0, keepdims=True)
            first = jnp.min(jnp.where(gsc == mx, blk_id, GATE_WIDTH), axis=0, keepdims=True)
            pick = jnp.logical_and(blk_id == first, mx > -jnp.inf)
            sel = jnp.logical_or(sel, pick)
            gsc = jnp.where(pick, -jnp.inf, gsc)
        sel_rows.append(jnp.where(sel, 1.0, 0.0))
    sel_q = jnp.transpose(jnp.concatenate(sel_rows, axis=0))
    conv_chunks(n_chunks // 2, 3 * n_chunks // 4)

    lane_a = lax.broadcasted_iota(jnp.int32, (tm, N_HEADS * AUG), 1) % AUG
    in_gate = jnp.logical_and(lane_a >= GATE_LANE0, lane_a < GATE_LANE0 + GATE_WIDTH)
    qsel = jnp.concatenate([(q * (1.0 / math.sqrt(HEAD_DIM))).astype(BF16), sel_q.astype(BF16)], axis=1)
    qa = jnp.dot(qsel, pq_ref[...], preferred_element_type=F32)
    qaug_ref[0, 0] = jnp.where(in_gate, jnp.where(qa > 0.5, 0.0, -MASK_BIG), qa).astype(BF16)
    qaug_ref[1, 0] = jnp.where(in_gate, -MASK_BIG, qa).astype(BF16)

    row = lax.broadcasted_iota(jnp.int32, (tm, N_HEADS * AUG), 0)
    blk = jstep * nbt + jnp.right_shift(row, _log2(MOBA_BLOCK))
    kaug = jnp.dot(k.astype(BF16), pk_ref[...], preferred_element_type=F32)
    kaug_ref[0] = jnp.where(lane_a - GATE_LANE0 == blk, 1.0, kaug).astype(BF16)
    vaug = jnp.dot(p[:, 2 * g:].astype(BF16), pk_ref[...], preferred_element_type=F32)
    vaug_ref[0] = jnp.where(lane_a >= HEAD_DIM, 1.0, vaug).astype(BF16)

    conv_chunks(3 * n_chunks // 4, n_chunks)


def _inproj_call(x, shift, scale, g1, w_in, cos_t, sin_t, gq, gk, pk, eh, pq, ws, wc, bc, gl, bl, tm):
    b, s, d = x.shape
    g = GROUP
    wa = N_HEADS * AUG
    row = lambda w: pl.BlockSpec((1, tm, w), lambda i, j: (i, j, 0))
    mod = pl.BlockSpec((1, 1, d), lambda i, j: (i, 0, 0))
    bf16o = lambda w: jax.ShapeDtypeStruct((b, s, w), BF16)
    return pl.pallas_call(
        _inproj_kernel,
        grid=(b, s // tm),
        in_specs=[row(d), mod, mod, _const_spec((1, d)), _const_spec(w_in.shape), row(LANES), row(LANES),
                  _const_spec((1, g)), _const_spec((1, g)), _const_spec(pk.shape), _const_spec(eh.shape),
                  _const_spec(pq.shape), _const_spec(ws.shape), _const_spec(wc.shape), _const_spec((1, g)),
                  _const_spec((1, g)), _const_spec((1, g))],
        out_specs=[row(g), row(g), row(g), row(g), row(g),
                   pl.BlockSpec((2, 1, tm, wa), lambda i, j: (0, i, j, 0)), row(wa), row(wa)],
        out_shape=[bf16o(g), bf16o(g), bf16o(g), bf16o(g), bf16o(g), jax.ShapeDtypeStruct((2, b, s, wa), BF16),
                   bf16o(wa), bf16o(wa)],
        scratch_shapes=[pltpu.VMEM((GATE_WIDTH, g), F32), pltpu.VMEM((tm, g), F32),
                        pltpu.VMEM((tm + SHORT_HALO, g), F32), pltpu.VMEM((tm + CONV_HALO, g), F32),
                        pltpu.VMEM((SUBLANES, tm + CONV_HALO, g), F32)],
        compiler_params=_cparams(("arbitrary", "arbitrary")),
        name="in_proj",
    )(x, shift, scale, g1, w_in, cos_t, sin_t, gq, gk, pk, eh, pq, ws, wc, bc, gl, bl)


def _softplus(z):
    return jnp.maximum(z, 0.0) + jnp.log(1.0 + jnp.exp(-jnp.abs(z)))


def _suffix_sum(l, upper):
    hi, lo = _split_bf16(l)
    return (jnp.dot(hi, upper, preferred_element_type=F32) + jnp.dot(lo, upper, preferred_element_type=F32))


def _sb_kernel(q_ref, k_ref, v_ref, o_ref, r_ref, acc_ref, z_ref, c_ref):
    t = SB_BLOCK
    w = SB_WINDOW * t
    n_sub = q_ref.shape[1] // t
    step = pl.program_id(1)
    ur = lax.broadcasted_iota(jnp.int32, (w, w), 0)
    uc = lax.broadcasted_iota(jnp.int32, (w, w), 1)
    upper_w = (ur >= uc).astype(BF16)
    upper = upper_w[:t, :t]
    low_head = lax.broadcasted_iota(jnp.int32, (t, t), 1) < HEAD_DIM
    nt = (((1,), (1,)), ((), ()))

    def qmask(u, h):
        pair = q_ref[0, u * t:(u + 1) * t, (h // 2) * LANES:(h // 2 + 1) * LANES]
        keep = low_head if h % 2 == 0 else jnp.logical_not(low_head)
        return jnp.where(keep, pair, jnp.zeros_like(pair))

    def kv(ref, start, n, h):
        return ref[0, pl.ds(start, n), (h // 2) * LANES:(h // 2 + 1) * LANES]

    chains = [(u, h) for u in range(n_sub) for h in range(N_HEADS)]
    first_tile, wstarts, valids = [], [], []
    for u in range(n_sub):
        i = step * n_sub + u
        first = jnp.maximum(i - (SB_WINDOW - 1), 0)
        wstart = pl.multiple_of(first * t, t)
        kpos = wstart + lax.broadcasted_iota(jnp.int32, (t, w), 1)
        qpos = i * t + lax.broadcasted_iota(jnp.int32, (t, w), 0)
        first_tile.append(first)
        wstarts.append(wstart)
        valids.append(kpos < qpos)
    for u, h in chains:
        z_ref[u, h] = lax.dot_general(qmask(u, h), kv(k_ref, wstarts[u], w, h), nt, preferred_element_type=F32)
    for u, h in chains:
        l = jnp.where(valids[u], -_softplus(z_ref[u, h]), 0.0)
        c_ref[u, h] = _suffix_sum(l, upper_w)
    rmax0 = [jnp.float32(-jnp.inf)] * n_sub
    for u, h in chains:
        c = c_ref[u, h]
        a = jnp.where(valids[u], jnp.exp(z_ref[u, h] + c), 0.0)
        acc_ref[u, h] = jnp.dot(a.astype(BF16), kv(v_ref, wstarts[u], w, h), preferred_element_type=F32)
        r = jnp.broadcast_to(c[:, 0:1], (t, t))
        r_ref[u, h] = r
        rmax0[u] = jnp.maximum(rmax0[u], jnp.max(r))

    def cond(carry):
        j, rmax = carry
        return jnp.logical_and(j >= 0, rmax > SB_EXIT)

    for u in range(n_sub):
        def body(carry, u=u):
            j, _ = carry
            start = pl.multiple_of(j * t, t)
            rmax = jnp.float32(-jnp.inf)
            for h in range(N_HEADS):
                z_ref[u, h, :, 0:t] = lax.dot_general(qmask(u, h), kv(k_ref, start, t, h), nt,
                                                      preferred_element_type=F32)
            for h in range(N_HEADS):
                c_ref[u, h, :, 0:t] = _suffix_sum(-_softplus(z_ref[u, h, :, 0:t]), upper)
            for h in range(N_HEADS):
                r = r_ref[u, h]
                c = c_ref[u, h, :, 0:t]
                a = jnp.exp(z_ref[u, h, :, 0:t] + c + r)
                acc_ref[u, h] = acc_ref[u, h] + jnp.dot(a.astype(BF16), kv(v_ref, start, t, h),
                                                        preferred_element_type=F32)
                r_new = r + jnp.broadcast_to(c[:, 0:1], (t, t))
                r_ref[u, h] = r_new
                rmax = jnp.maximum(rmax, jnp.max(r_new))
            return j - 1, rmax

        lax.while_loop(cond, body, (first_tile[u] - 1, rmax0[u]))

    for u in range(n_sub):
        for p in range(N_HEADS // 2):
            o_ref[0, u * t:(u + 1) * t, p * LANES:(p + 1) * LANES] = jnp.where(
                low_head, acc_ref[u, 2 * p], acc_ref[u, 2 * p + 1]).astype(BF16)


SB_STEP = 256
SB_WINDOW = 3


def _sb_call(q, k, v):
    b, s, g = q.shape
    t = SB_BLOCK
    n_sub = SB_STEP // t
    seq = pl.BlockSpec((1, s, g), lambda i, j: (i, 0, 0))
    tile = pl.BlockSpec((1, SB_STEP, g), lambda i, j: (i, j, 0))
    return pl.pallas_call(
        _sb_kernel,
        grid=(b, s // SB_STEP),
        in_specs=[tile, seq, seq],
        out_specs=tile,
        out_shape=jax.ShapeDtypeStruct((b, s, g), BF16),
        scratch_shapes=[pltpu.VMEM((n_sub, N_HEADS, t, t), F32), pltpu.VMEM((n_sub, N_HEADS, t, t), F32),
                        pltpu.VMEM((n_sub, N_HEADS, t, SB_WINDOW * t), F32),
                        pltpu.VMEM((n_sub, N_HEADS, t, SB_WINDOW * t), F32)],
        compiler_params=_cparams(("arbitrary", "arbitrary")),
        name="stick_breaking",
    )(q, k, v)


def _moba_kernel(qa_ref, kaug_ref, vaug_ref, po_ref, o_ref, m_ref, acc_ref, sa_ref, sb_ref):
    t = MOBA_Q_TILE
    tk = MOBA_KV_TILE
    i = pl.program_id(1)
    lane = lax.broadcasted_iota(jnp.int32, (t, AUG), 1)
    nt = (((1,), (1,)), ((), ()))

    def kblk(ref, start, n, h):
        return ref[0, pl.ds(start, n), h * AUG:(h + 1) * AUG]

    for h in range(N_HEADS):
        m_ref[h] = jnp.full((t, AUG), -jnp.inf, F32)
        acc_ref[h] = jnp.zeros((t, AUG), F32)

    def scores(j, dst_ref, reject):
        start = pl.multiple_of(j * tk, tk)
        for h in range(N_HEADS):
            dst_ref[h] = lax.dot_general(qa_ref[reject, 0, :, h * AUG:(h + 1) * AUG],
                                         kblk(kaug_ref, start, tk, h), nt, preferred_element_type=F32)

    def softmax_pv(j, src_ref, mask=None):
        start = pl.multiple_of(j * tk, tk)
        for h in range(N_HEADS):
            s = src_ref[h]
            if mask is not None:
                s = jnp.where(mask, s, -jnp.inf)
            m_old = m_ref[h]
            m_new = jnp.maximum(m_old, jnp.max(s, axis=-1, keepdims=True))
            alpha = jnp.exp(m_old - m_new)
            p = jnp.exp(s - jnp.concatenate([m_new] * (tk // AUG), axis=1))
            acc_ref[h] = acc_ref[h] * alpha + jnp.dot(p.astype(BF16), kblk(vaug_ref, start, tk, h),
                                                      preferred_element_type=F32)
            m_ref[h] = m_new

    n_past = jnp.right_shift(i * t, _log2(tk))
    last = jnp.maximum(n_past - 1, 0)
    reject = lambda j: (j >= n_past).astype(jnp.int32)
    drow = lax.broadcasted_iota(jnp.int32, (t, tk), 0)
    dcol = lax.broadcasted_iota(jnp.int32, (t, tk), 1)
    diag_mask = n_past * tk + dcol <= i * t + drow
    scores(n_past, sa_ref, 0)
    scores(0, sb_ref, reject(0))
    softmax_pv(n_past, sa_ref, diag_mask)

    def body(jj, carry):
        a = 2 * jj
        scores(jnp.minimum(a + 1, last), sa_ref, reject(a + 1))
        softmax_pv(a, sb_ref)
        scores(jnp.minimum(a + 2, last), sb_ref, reject(a + 2))
        softmax_pv(jnp.minimum(a + 1, last), sa_ref)
        return carry

    lax.fori_loop(0, jnp.right_shift(n_past + 1, 1), body, 0)

    out = jnp.zeros((t, GROUP), F32)
    for h in range(N_HEADS):
        acc = acc_ref[h]
        denom = pltpu.roll(acc, HEAD_DIM, 1)
        y = jnp.where(lane < HEAD_DIM, acc / denom, 0.0).astype(BF16)
        out = out + jnp.dot(y, po_ref[h], preferred_element_type=F32)
    o_ref[0] = out.astype(BF16)


def _moba_call(qaug, kaug, vaug, po):
    _, b, s, wa = qaug.shape
    g = GROUP
    t = MOBA_Q_TILE
    assert MOBA_KV_TILE % t == 0 and s % MOBA_KV_TILE == 0
    seq = pl.BlockSpec((1, s, wa), lambda i, j: (i, 0, 0))
    tile = pl.BlockSpec((1, t, g), lambda i, j: (i, j, 0))
    return pl.pallas_call(
        _moba_kernel,
        grid=(b, s // t),
        in_specs=[pl.BlockSpec((2, 1, t, wa), lambda i, j: (0, i, j, 0)), seq, seq, _const_spec(po.shape)],
        out_specs=tile,
        out_shape=jax.ShapeDtypeStruct((b, s, g), BF16),
        scratch_shapes=[pltpu.VMEM((N_HEADS, t, AUG), F32), pltpu.VMEM((N_HEADS, t, AUG), F32),
                        pltpu.VMEM((N_HEADS, t, MOBA_KV_TILE), F32), pltpu.VMEM((N_HEADS, t, MOBA_KV_TILE), F32)],
        compiler_params=_cparams(("arbitrary", "arbitrary")),
        name="moba",
    )(qaug, kaug, vaug, po)


MLP_CHUNK = 512


def _outmlp_kernel(x_ref, ya_ref, yb_ref, yc_ref, yd_ref, g1_ref, sh_ref, sc_ref, g2_ref, gn_ref, wo_ref,
                   w1_ref, w2_ref, o_ref):
    g = GROUP
    mix = jnp.dot(ya_ref[0], wo_ref[0:g], preferred_element_type=F32)
    mix = mix + jnp.dot(yb_ref[0], wo_ref[g:2 * g], preferred_element_type=F32)
    mix = mix + jnp.dot(yc_ref[0], wo_ref[2 * g:3 * g], preferred_element_type=F32)
    mix = mix + jnp.dot(yd_ref[0], wo_ref[3 * g:4 * g], preferred_element_type=F32)
    x1 = x_ref[0] + g1_ref[0] * mix
    ms = jnp.mean(x1 * x1, axis=-1, keepdims=True)
    h = x1 * lax.rsqrt(ms + EPS) * gn_ref[...]
    hb = (h * (1.0 + sc_ref[0]) + sh_ref[0]).astype(BF16)
    acc = jnp.zeros(x1.shape, F32)
    for f0 in range(0, D_FF, MLP_CHUNK):
        a = jnp.maximum(jnp.dot(hb, w1_ref[:, f0:f0 + MLP_CHUNK], preferred_element_type=F32), 0.0)
        acc = acc + jnp.dot((a * a).astype(BF16), w2_ref[f0:f0 + MLP_CHUNK, :], preferred_element_type=F32)
    o_ref[0] = x1 + g2_ref[0] * acc


def _outmlp_call(x, ya, yb, yc, yd, gate1, shift2, scale2, gate2, gn2, wo, w1, w2, tm):
    b, s, d = x.shape
    row = lambda w: pl.BlockSpec((1, tm, w), lambda i, j: (i, j, 0))
    mod = pl.BlockSpec((1, 1, d), lambda i, j: (i, 0, 0))
    return pl.pallas_call(
        _outmlp_kernel,
        grid=(b, s // tm),
        in_specs=[row(d), row(GROUP), row(GROUP), row(GROUP), row(GROUP), mod, mod, mod, mod,
                  _const_spec((1, d)), _const_spec(wo.shape), _const_spec(w1.shape), _const_spec(w2.shape)],
        out_specs=row(d),
        out_shape=jax.ShapeDtypeStruct((b, s, d), F32),
        compiler_params=_cparams(("arbitrary", "arbitrary")),
        name="out_mlp",
    )(x, ya, yb, yc, yd, gate1, shift2, scale2, gate2, gn2, wo, w1, w2)


def _placement_matrices():
    src = jnp.arange(GROUP)
    dst = (src // HEAD_DIM) * AUG + src % HEAD_DIM
    pk = jnp.zeros((GROUP, N_HEADS * AUG), BF16).at[src, dst].set(1.0)
    head = jnp.arange(N_HEADS)[:, None, None]
    r = jnp.arange(GROUP)[None, :, None]
    c = jnp.arange(AUG)[None, None, :]
    po = ((r // HEAD_DIM == head) & (c == r % HEAD_DIM)).astype(BF16)
    po = jnp.transpose(po, (0, 2, 1))
    rs = jnp.arange(N_HEADS * GATE_WIDTH)
    dst_s = (rs // GATE_WIDTH) * AUG + GATE_LANE0 + rs % GATE_WIDTH
    pq = jnp.concatenate([pk, jnp.zeros((N_HEADS * GATE_WIDTH, N_HEADS * AUG), BF16).at[rs, dst_s].set(1.0)],
                         axis=0)
    eh = (src[:, None] // HEAD_DIM == src[None, :] // HEAD_DIM).astype(BF16)
    return pk, pq, po, eh


def _pick_tile(s, pref):
    t = pref
    while s % t:
        t //= 2
    return t


def kernel(x, c, positions, w_ada, b_ada, g_norm1, w_in, w_sconv, w_cconv, b_cconv, g_cln, b_cln, g_q, g_k,
           w_out, g_norm2, w_mlp1, w_mlp2):
    b, s, d = x.shape
    depth = w_ada.shape[0]
    nb = s // MOBA_BLOCK
    assert d == D_MODEL and s % MOBA_KV_TILE == 0 and nb <= GATE_WIDTH
    tm = _pick_tile(s, 512)
    pk, pq, po, eh = _placement_matrices()
    mod = _ada_call(c, w_ada, b_ada).reshape(depth, b, 6, 1, d)
    cos_t, sin_t = _rope_call(positions)
    tile4 = lambda v: jnp.tile(v, N_HEADS).reshape(1, GROUP)
    for l in range(depth):
        shift1, scale1, gate1, shift2, scale2, gate2 = (mod[l, :, i] for i in range(6))
        ya, yd, sbq, sbk, sbv, qaug, kaug, vaug = _inproj_call(
            x, shift1, scale1, g_norm1[l].reshape(1, d), w_in[l].astype(BF16), cos_t, sin_t,
            tile4(g_q[l]), tile4(g_k[l]), pk, eh, pq, w_sconv[l], w_cconv[l], b_cconv[l].reshape(1, GROUP),
            g_cln[l].reshape(1, GROUP), b_cln[l].reshape(1, GROUP), tm)
        yb = _sb_call(sbq, sbk, sbv)
        yc = _moba_call(qaug, kaug, vaug, po)
        x = _outmlp_call(x, ya, yb, yc, yd, gate1, shift2, scale2, gate2, g_norm2[l].reshape(1, d),
                         w_out[l].astype(BF16), w_mlp1[l].astype(BF16), w_mlp2[l].astype(BF16), tm)
    return x
```

```python
import functools
import math

import jax
import jax.numpy as jnp
from jax import lax
from jax.experimental import pallas as pl
from jax.experimental.pallas import tpu as pltpu

F32 = jnp.float32
BF16 = jnp.bfloat16
HIGHEST = lax.Precision.HIGHEST

D_MODEL = 1024
GROUP = 256
HEAD_DIM = 64
N_HEADS = GROUP // HEAD_DIM
IN_WIDTH = 11 * GROUP
D_FF = 4 * D_MODEL
SHORT_K = 3
CONF_K = 31
CONV_HALO = 32
SHORT_HALO = 8
CONV_CHUNK = 64
SB_BLOCK = 128
MOBA_BLOCK = 256
MOBA_TOPK = 3
MOBA_KV_TILE = 512
MOBA_Q_TILE = 512
ROPE_THETA = 10000.0
EPS = 1e-6
LANES = 128
SUBLANES = 8
AUG = 128
GATE_LANE0 = HEAD_DIM
GATE_WIDTH = 32
MASK_BIG = 2.0 ** 100
SB_EXIT = -90.0
VMEM_LIMIT = 56 * 1024 * 1024


def _log2(n):
    assert n & (n - 1) == 0
    return n.bit_length() - 1


def _cparams(sem):
    return pltpu.CompilerParams(dimension_semantics=sem, vmem_limit_bytes=VMEM_LIMIT)


def _split_bf16(a):
    hi = a.astype(BF16)
    return hi, (a - hi.astype(F32)).astype(BF16)


def _const_spec(shape):
    n = len(shape)
    return pl.BlockSpec(shape, lambda *_: (0,) * n, pipeline_mode=pl.Buffered(1))


def _ada_kernel(c_ref, w_ref, b_ref, o_ref):
    c = c_ref[...]
    ca = c * jax.nn.sigmoid(c)
    o_ref[0] = jnp.dot(ca, w_ref[0], precision=HIGHEST, preferred_element_type=F32) + b_ref[0]


def _ada_call(c, w_ada, b_ada):
    depth, d, n = w_ada.shape
    b = c.shape[0]
    tn = 1536
    return pl.pallas_call(
        _ada_kernel,
        grid=(depth, n // tn),
        in_specs=[pl.BlockSpec((b, d), lambda l, j: (0, 0)),
                  pl.BlockSpec((1, d, tn), lambda l, j: (l, 0, j)),
                  pl.BlockSpec((1, 1, tn), lambda l, j: (l, 0, j))],
        out_specs=pl.BlockSpec((1, b, tn), lambda l, j: (l, 0, j)),
        out_shape=jax.ShapeDtypeStruct((depth, b, n), F32),
        compiler_params=_cparams(("arbitrary", "arbitrary")),
        name="ada_mod",
    )(c, w_ada, b_ada.reshape(depth, 1, n))


def _rope_kernel(pos_ref, cos_ref, sin_ref):
    pos = pos_ref[0].astype(F32)
    lane = lax.broadcasted_iota(jnp.int32, (1, LANES), 1)
    half = HEAD_DIM // 2
    idx = (lane % half).astype(F32)
    inv_freq = jnp.exp((-math.log(ROPE_THETA) * idx) / half)
    ang = pos * inv_freq
    sign = jnp.where((lane % HEAD_DIM) < half, -1.0, 1.0)
    cos_ref[0] = jnp.cos(ang)
    sin_ref[0] = jnp.sin(ang) * sign


def _rope_call(positions):
    b, s = positions.shape
    ts = 1024
    spec = pl.BlockSpec((1, ts, LANES), lambda i, j: (i, j, 0))
    return pl.pallas_call(
        _rope_kernel,
        grid=(b, s // ts),
        in_specs=[pl.BlockSpec((1, ts, 1), lambda i, j: (i, j, 0))],
        out_specs=[spec, spec],
        out_shape=[jax.ShapeDtypeStruct((b, s, LANES), F32)] * 2,
        compiler_params=_cparams(("arbitrary", "arbitrary")),
        name="rope_table",
    )(positions.reshape(b, s, 1))


def _inproj_kernel(x_ref, sh_ref, sc_ref, g1_ref, w_ref, cos_ref, sin_ref, gq_ref, gk_ref, pk_ref,
                   eh_ref, pq_ref, ws_ref, wc_ref, bc_ref, gl_ref, bl_ref,
                   ya_ref, yd_ref, sbq_ref, sbk_ref, sbv_ref, qaug_ref, kaug_ref, vaug_ref,
                   km_scr, scb_scr, exts, extc, shc):
    tm = x_ref.shape[1]
    g = GROUP
    nbt = tm // MOBA_BLOCK
    jstep = pl.program_id(1)

    @pl.when(jstep == 0)
    def _():
        km_scr[...] = jnp.zeros(km_scr.shape, F32)
        exts[0:SHORT_HALO] = jnp.zeros((SHORT_HALO, g), F32)
        extc[0:CONV_HALO] = jnp.zeros((CONV_HALO, g), F32)

    @pl.when(jstep > 0)
    def _():
        exts[0:SHORT_HALO] = exts[tm:tm + SHORT_HALO]
        extc[0:CONV_HALO] = extc[tm:tm + CONV_HALO]

    x = x_ref[0]
    ms = jnp.mean(x * x, axis=-1, keepdims=True)
    h = x * lax.rsqrt(ms + EPS) * g1_ref[...]
    h = h * (1.0 + sc_ref[0]) + sh_ref[0]
    hb = h.astype(BF16)

    def proj(c0, n):
        return jnp.dot(hb, w_ref[:, c0:c0 + n], preferred_element_type=F32)

    p = proj(0, 3 * g)
    scb_scr[...] = p[:, :g]
    exts[SHORT_HALO:SHORT_HALO + tm] = p[:, g:2 * g] * p[:, 2 * g:]

    p = proj(9 * g, 2 * g)
    extc[CONV_HALO:CONV_HALO + tm] = p[:, :g] * jax.nn.sigmoid(p[:, g:])

    n_ext = tm + CONV_HALO
    for r in range(SUBLANES):
        n = n_ext - (SUBLANES if r else 0)
        shc[r, 0:n] = extc[r:r + n]

    def conv_chunks(first, stop):
        for c0 in range(first * CONV_CHUNK, stop * CONV_CHUNK, CONV_CHUNK):
            acc = jnp.zeros((CONV_CHUNK, g), F32)
            for kk in range(SHORT_K):
                o = SHORT_HALO - (SHORT_K - 1) + kk + c0
                acc = acc + ws_ref[kk:kk + 1, :] * exts[o:o + CONV_CHUNK]
            ya_ref[0, c0:c0 + CONV_CHUNK] = (scb_scr[c0:c0 + CONV_CHUNK] * acc).astype(BF16)

            acc = jnp.zeros((CONV_CHUNK, g), F32)
            for kk in range(CONF_K):
                o = CONV_HALO - (CONF_K - 1) + kk
                a0 = o - o % SUBLANES + c0
                acc = acc + wc_ref[kk:kk + 1, :] * shc[o % SUBLANES, a0:a0 + CONV_CHUNK]
            u = acc + bc_ref[...]
            mu = jnp.mean(u, axis=-1, keepdims=True)
            var = jnp.mean(jnp.square(u - mu), axis=-1, keepdims=True)
            y = (u - mu) * lax.rsqrt(var + EPS) * gl_ref[...] + bl_ref[...]
            yd_ref[0, c0:c0 + CONV_CHUNK] = (y * jax.nn.sigmoid(y)).astype(BF16)

    n_chunks = tm // CONV_CHUNK

    p = proj(3 * g, 3 * g)
    sbq_ref[0] = (p[:, :g] * (1.0 / math.sqrt(HEAD_DIM))).astype(BF16)
    sbk_ref[0] = p[:, g:2 * g].astype(BF16)
    sbv_ref[0] = p[:, 2 * g:].astype(BF16)
    conv_chunks(0, n_chunks // 4)

    p = proj(6 * g, 3 * g)
    cos_t = cos_ref[0]
    sin_t = sin_ref[0]
    lane = lax.broadcasted_iota(jnp.int32, (tm, LANES), 1)
    first_half = (lane % HEAD_DIM) < (HEAD_DIM // 2)
    half = HEAD_DIM // 2

    def norm_rope(t, gain):
        sq_hi, sq_lo = _split_bf16(t * t)
        ss = (jnp.dot(sq_hi, eh_ref[...], preferred_element_type=F32)
              + jnp.dot(sq_lo, eh_ref[...], preferred_element_type=F32))
        y = t * lax.rsqrt(ss * (1.0 / HEAD_DIM) + EPS) * gain
        out = []
        for c0 in range(0, g, LANES):
            yh = y[:, c0:c0 + LANES]
            partner = jnp.where(first_half, pltpu.roll(yh, LANES - half, 1), pltpu.roll(yh, half, 1))
            out.append(yh * cos_t + partner * sin_t)
        return jnp.concatenate(out, axis=1)

    q = norm_rope(p[:, :g], gq_ref[...])
    k = norm_rope(p[:, g:2 * g], gk_ref[...])
    conv_chunks(n_chunks // 4, n_chunks // 2)

    for r in range(nbt):
        km_scr[pl.ds(jstep * nbt + r, 1), :] = jnp.mean(k[r * MOBA_BLOCK:(r + 1) * MOBA_BLOCK], axis=0,
                                                        keepdims=True)

    nt = (((1,), (1,)), ((), ()))
    gr = lax.broadcasted_iota(jnp.int32, (N_HEADS * GATE_WIDTH, g), 0)
    gc = lax.broadcasted_iota(jnp.int32, (N_HEADS * GATE_WIDTH, g), 1)
    same_head = jnp.right_shift(gr, _log2(GATE_WIDTH)) == jnp.right_shift(gc, _log2(HEAD_DIM))
    km4 = jnp.where(same_head, jnp.concatenate([km_scr[...]] * N_HEADS, axis=0), 0.0)
    k_hi, k_lo = _split_bf16(km4)
    q_hi, q_lo = _split_bf16(q)
    gate_t = (lax.dot_general(k_hi, q_hi, nt, preferred_element_type=F32)
              + lax.dot_general(k_hi, q_lo, nt, preferred_element_type=F32)
              + lax.dot_general(k_lo, q_hi, nt, preferred_element_type=F32))
    blk_id = lax.broadcasted_iota(jnp.int32, (GATE_WIDTH, tm), 0)
    own = jstep * nbt + jnp.right_shift(lax.broadcasted_iota(jnp.int32, (GATE_WIDTH, tm), 1), _log2(MOBA_BLOCK))
    sel_rows = []
    for hh in range(N_HEADS):
        gsc = jnp.where(blk_id < own, gate_t[hh * GATE_WIDTH:(hh + 1) * GATE_WIDTH], -jnp.inf)
        sel = blk_id == own
        for _ in range(MOBA_TOPK):
            mx = jnp.max(gsc, axis=0, keepdims=True)
            first = jnp.min(jnp.where(gsc == mx, blk_id, GATE_WIDTH), axis=0, keepdims=True)
            pick = jnp.logical_and(blk_id == first, mx > -jnp.inf)
            sel = jnp.logical_or(sel, pick)
            gsc = jnp.where(pick, -jnp.inf, gsc)
        sel_rows.append(jnp.where(sel, 1.0, 0.0))
    sel_q = jnp.transpose(jnp.concatenate(sel_rows, axis=0))
    conv_chunks(n_chunks // 2, 3 * n_chunks // 4)

    lane_a = lax.broadcasted_iota(jnp.int32, (tm, N_HEADS * AUG), 1) % AUG
    in_gate = jnp.logical_and(lane_a >= GATE_LANE0, lane_a < GATE_LANE0 + GATE_WIDTH)
    qsel = jnp.concatenate([(q * (1.0 / math.sqrt(HEAD_DIM))).astype(BF16), sel_q.astype(BF16)], axis=1)
    qa = jnp.dot(qsel, pq_ref[...], preferred_element_type=F32)
    qaug_ref[0] = jnp.where(in_gate, jnp.where(qa > 0.5, 0.0, -MASK_BIG), qa).astype(BF16)

    row = lax.broadcasted_iota(jnp.int32, (tm, N_HEADS * AUG), 0)
    blk = jstep * nbt + jnp.right_shift(row, _log2(MOBA_BLOCK))
    kaug = jnp.dot(k.astype(BF16), pk_ref[...], preferred_element_type=F32)
    kaug_ref[0] = jnp.where(lane_a - GATE_LANE0 == blk, 1.0, kaug).astype(BF16)
    vaug = jnp.dot(p[:, 2 * g:].astype(BF16), pk_ref[...], preferred_element_type=F32)
    vaug_ref[0] = jnp.where(lane_a >= HEAD_DIM, 1.0, vaug).astype(BF16)

    conv_chunks(3 * n_chunks // 4, n_chunks)


def _inproj_call(x, shift, scale, g1, w_in, cos_t, sin_t, gq, gk, pk, eh, pq, ws, wc, bc, gl, bl, tm):
    b, s, d = x.shape
    g = GROUP
    wa = N_HEADS * AUG
    row = lambda w: pl.BlockSpec((1, tm, w), lambda i, j: (i, j, 0))
    mod = pl.BlockSpec((1, 1, d), lambda i, j: (i, 0, 0))
    bf16o = lambda w: jax.ShapeDtypeStruct((b, s, w), BF16)
    return pl.pallas_call(
        _inproj_kernel,
        grid=(b, s // tm),
        in_specs=[row(d), mod, mod, _const_spec((1, d)), _const_spec(w_in.shape), row(LANES), row(LANES),
                  _const_spec((1, g)), _const_spec((1, g)), _const_spec(pk.shape), _const_spec(eh.shape),
                  _const_spec(pq.shape), _const_spec(ws.shape), _const_spec(wc.shape), _const_spec((1, g)),
                  _const_spec((1, g)), _const_spec((1, g))],
        out_specs=[row(g), row(g), row(g), row(g), row(g),
                   row(wa), row(wa), row(wa)],
        out_shape=[bf16o(g), bf16o(g), bf16o(g), bf16o(g), bf16o(g), bf16o(wa), bf16o(wa), bf16o(wa)],
        scratch_shapes=[pltpu.VMEM((GATE_WIDTH, g), F32), pltpu.VMEM((tm, g), F32),
                        pltpu.VMEM((tm + SHORT_HALO, g), F32), pltpu.VMEM((tm + CONV_HALO, g), F32),
                        pltpu.VMEM((SUBLANES, tm + CONV_HALO, g), F32)],
        compiler_params=_cparams(("arbitrary", "arbitrary")),
        name="in_proj",
    )(x, shift, scale, g1, w_in, cos_t, sin_t, gq, gk, pk, eh, pq, ws, wc, bc, gl, bl)


def _softplus(z):
    return jnp.maximum(z, 0.0) + jnp.log(1.0 + jnp.exp(-jnp.abs(z)))


def _suffix_sum(l, upper):
    hi, lo = _split_bf16(l)
    return (jnp.dot(hi, upper, preferred_element_type=F32) + jnp.dot(lo, upper, preferred_element_type=F32))


def _sb_kernel(q_ref, k_ref, v_ref, o_ref, r_ref, acc_ref, z_ref, c_ref):
    t = SB_BLOCK
    w = SB_WINDOW * t
    n_sub = q_ref.shape[1] // t
    step = pl.program_id(1)
    ur = lax.broadcasted_iota(jnp.int32, (w, w), 0)
    uc = lax.broadcasted_iota(jnp.int32, (w, w), 1)
    upper_w = (ur >= uc).astype(BF16)
    upper = upper_w[:t, :t]
    low_head = lax.broadcasted_iota(jnp.int32, (t, t), 1) < HEAD_DIM
    nt = (((1,), (1,)), ((), ()))

    def qmask(u, h):
        pair = q_ref[0, u * t:(u + 1) * t, (h // 2) * LANES:(h // 2 + 1) * LANES]
        keep = low_head if h % 2 == 0 else jnp.logical_not(low_head)
        return jnp.where(keep, pair, jnp.zeros_like(pair))

    def kv(ref, start, n, h):
        return ref[0, pl.ds(start, n), (h // 2) * LANES:(h // 2 + 1) * LANES]

    chains = [(u, h) for u in range(n_sub) for h in range(N_HEADS)]
    first_tile, wstarts, valids = [], [], []
    for u in range(n_sub):
        i = step * n_sub + u
        first = jnp.maximum(i - (SB_WINDOW - 1), 0)
        wstart = pl.multiple_of(first * t, t)
        kpos = wstart + lax.broadcasted_iota(jnp.int32, (t, w), 1)
        qpos = i * t + lax.broadcasted_iota(jnp.int32, (t, w), 0)
        first_tile.append(first)
        wstarts.append(wstart)
        valids.append(kpos < qpos)
    for u, h in chains:
        z_ref[u, h] = lax.dot_general(qmask(u, h), kv(k_ref, wstarts[u], w, h), nt, preferred_element_type=F32)
    for u, h in chains:
        l = jnp.where(valids[u], -_softplus(z_ref[u, h]), 0.0)
        c_ref[u, h] = _suffix_sum(l, upper_w)
    rmax0 = [jnp.float32(-jnp.inf)] * n_sub
    for u, h in chains:
        c = c_ref[u, h]
        a = jnp.where(valids[u], jnp.exp(z_ref[u, h] + c), 0.0)
        acc_ref[u, h] = jnp.dot(a.astype(BF16), kv(v_ref, wstarts[u], w, h), preferred_element_type=F32)
        r = jnp.broadcast_to(c[:, 0:1], (t, t))
        r_ref[u, h] = r
        rmax0[u] = jnp.maximum(rmax0[u], jnp.max(r))

    def cond(carry):
        j, rmax = carry
        return jnp.logical_and(j >= 0, rmax > SB_EXIT)

    for u in range(n_sub):
        def body(carry, u=u):
            j, _ = carry
            start = pl.multiple_of(j * t, t)
            rmax = jnp.float32(-jnp.inf)
            for h in range(N_HEADS):
                z_ref[u, h, :, 0:t] = lax.dot_general(qmask(u, h), kv(k_ref, start, t, h), nt,
                                                      preferred_element_type=F32)
            for h in range(N_HEADS):
                c_ref[u, h, :, 0:t] = _suffix_sum(-_softplus(z_ref[u, h, :, 0:t]), upper)
            for h in range(N_HEADS):
                r = r_ref[u, h]
                c = c_ref[u, h, :, 0:t]
                a = jnp.exp(z_ref[u, h, :, 0:t] + c + r)
                acc_ref[u, h] = acc_ref[u, h] + jnp.dot(a.astype(BF16), kv(v_ref, start, t, h),
                                                        preferred_element_type=F32)
                r_new = r + jnp.broadcast_to(c[:, 0:1], (t, t))
                r_ref[u, h] = r_new
                rmax = jnp.maximum(rmax, jnp.max(r_new))
            return j - 1, rmax

        lax.while_loop(cond, body, (first_tile[u] - 1, rmax0[u]))

    for u in range(n_sub):
        for p in range(N_HEADS // 2):
            o_ref[0, u * t:(u + 1) * t, p * LANES:(p + 1) * LANES] = jnp.where(
                low_head, acc_ref[u, 2 * p], acc_ref[u, 2 * p + 1]).astype(BF16)


SB_STEP = 256
SB_WINDOW = 3


def _sb_call(q, k, v):
    b, s, g = q.shape
    t = SB_BLOCK
    n_sub = SB_STEP // t
    seq = pl.BlockSpec((1, s, g), lambda i, j: (i, 0, 0))
    tile = pl.BlockSpec((1, SB_STEP, g), lambda i, j: (i, j, 0))
    return pl.pallas_call(
        _sb_kernel,
        grid=(b, s // SB_STEP),
        in_specs=[tile, seq, seq],
        out_specs=tile,
        out_shape=jax.ShapeDtypeStruct((b, s, g), BF16),
        scratch_shapes=[pltpu.VMEM((n_sub, N_HEADS, t, t), F32), pltpu.VMEM((n_sub, N_HEADS, t, t), F32),
                        pltpu.VMEM((n_sub, N_HEADS, t, SB_WINDOW * t), F32),
                        pltpu.VMEM((n_sub, N_HEADS, t, SB_WINDOW * t), F32)],
        compiler_params=_cparams(("arbitrary", "arbitrary")),
        name="stick_breaking",
    )(q, k, v)


def _moba_kernel(qa_ref, kaug_ref, vaug_ref, po_ref, o_ref, m_ref, acc_ref, sa_ref, sb_ref):
    t = MOBA_Q_TILE
    tk = MOBA_KV_TILE
    i = pl.program_id(1)
    lane = lax.broadcasted_iota(jnp.int32, (t, AUG), 1)
    nt = (((1,), (1,)), ((), ()))

    def kblk(ref, start, n, h):
        return ref[0, pl.ds(start, n), h * AUG:(h + 1) * AUG]

    for h in range(N_HEADS):
        m_ref[h] = jnp.full((t, AUG), -jnp.inf, F32)
        acc_ref[h] = jnp.zeros((t, AUG), F32)

    def scores(j, dst_ref):
        start = pl.multiple_of(j * tk, tk)
        for h in range(N_HEADS):
            dst_ref[h] = lax.dot_general(qa_ref[0, :, h * AUG:(h + 1) * AUG],
                                         kblk(kaug_ref, start, tk, h), nt, preferred_element_type=F32)

    def softmax_pv(j, src_ref, mask=None):
        start = pl.multiple_of(j * tk, tk)
        for h in range(N_HEADS):
            s = src_ref[h]
            if mask is not None:
                s = jnp.where(mask, s, -jnp.inf)
            m_old = m_ref[h]
            m_new = jnp.maximum(m_old, jnp.max(s, axis=-1, keepdims=True))
            alpha = jnp.exp(m_old - m_new)
            p = jnp.exp(s - jnp.concatenate([m_new] * (tk // AUG), axis=1))
            acc_ref[h] = acc_ref[h] * alpha + jnp.dot(p.astype(BF16), kblk(vaug_ref, start, tk, h),
                                                      preferred_element_type=F32)
            m_ref[h] = m_new

    n_past = jnp.right_shift(i * t, _log2(tk))
    last = jnp.maximum(n_past - 1, 0)
    drow = lax.broadcasted_iota(jnp.int32, (t, tk), 0)
    dcol = lax.broadcasted_iota(jnp.int32, (t, tk), 1)
    diag_mask = n_past * tk + dcol <= i * t + drow
    scores(n_past, sa_ref)
    scores(0, sb_ref)
    softmax_pv(n_past, sa_ref, diag_mask)

    def body(jj, carry):
        a = 2 * jj
        scores(a + 1, sa_ref)
        softmax_pv(a, sb_ref)
        scores(jnp.minimum(a + 2, last), sb_ref)
        softmax_pv(a + 1, sa_ref)
        return carry

    lax.fori_loop(0, jnp.right_shift(n_past, 1), body, 0)

    @pl.when(jnp.bitwise_and(n_past, 1) == 1)
    def _():
        softmax_pv(last, sb_ref)

    out = jnp.zeros((t, GROUP), F32)
    for h in range(N_HEADS):
        acc = acc_ref[h]
        denom = pltpu.roll(acc, HEAD_DIM, 1)
        y = jnp.where(lane < HEAD_DIM, acc / denom, 0.0).astype(BF16)
        out = out + jnp.dot(y, po_ref[h], preferred_element_type=F32)
    o_ref[0] = out.astype(BF16)


def _moba_call(qaug, kaug, vaug, po):
    b, s, wa = qaug.shape
    g = GROUP
    t = MOBA_Q_TILE
    assert MOBA_KV_TILE % t == 0 and s % MOBA_KV_TILE == 0
    seq = pl.BlockSpec((1, s, wa), lambda i, j: (i, 0, 0))
    tile = pl.BlockSpec((1, t, g), lambda i, j: (i, j, 0))
    return pl.pallas_call(
        _moba_kernel,
        grid=(b, s // t),
        in_specs=[pl.BlockSpec((1, t, wa), lambda i, j: (i, j, 0)), seq, seq, _const_spec(po.shape)],
        out_specs=tile,
        out_shape=jax.ShapeDtypeStruct((b, s, g), BF16),
        scratch_shapes=[pltpu.VMEM((N_HEADS, t, AUG), F32), pltpu.VMEM((N_HEADS, t, AUG), F32),
                        pltpu.VMEM((N_HEADS, t, MOBA_KV_TILE), F32), pltpu.VMEM((N_HEADS, t, MOBA_KV_TILE), F32)],
        compiler_params=_cparams(("arbitrary", "arbitrary")),
        name="moba",
    )(qaug, kaug, vaug, po)


MLP_CHUNK = 512


def _outmlp_kernel(x_ref, ya_ref, yb_ref, yc_ref, yd_ref, g1_ref, sh_ref, sc_ref, g2_ref, gn_ref, wo_ref,
                   w1_ref, w2_ref, o_ref):
    g = GROUP
    mix = jnp.dot(ya_ref[0], wo_ref[0:g], preferred_element_type=F32)
    mix = mix + jnp.dot(yb_ref[0], wo_ref[g:2 * g], preferred_element_type=F32)
    mix = mix + jnp.dot(yc_ref[0], wo_ref[2 * g:3 * g], preferred_element_type=F32)
    mix = mix + jnp.dot(yd_ref[0], wo_ref[3 * g:4 * g], preferred_element_type=F32)
    x1 = x_ref[0] + g1_ref[0] * mix
    ms = jnp.mean(x1 * x1, axis=-1, keepdims=True)
    h = x1 * lax.rsqrt(ms + EPS) * gn_ref[...]
    hb = (h * (1.0 + sc_ref[0]) + sh_ref[0]).astype(BF16)
    acc = jnp.zeros(x1.shape, F32)
    for f0 in range(0, D_FF, MLP_CHUNK):
        a = jnp.maximum(jnp.dot(hb, w1_ref[:, f0:f0 + MLP_CHUNK], preferred_element_type=F32), 0.0)
        acc = acc + jnp.dot((a * a).astype(BF16), w2_ref[f0:f0 + MLP_CHUNK, :], preferred_element_type=F32)
    o_ref[0] = x1 + g2_ref[0] * acc


def _outmlp_call(x, ya, yb, yc, yd, gate1, shift2, scale2, gate2, gn2, wo, w1, w2, tm):
    b, s, d = x.shape
    row = lambda w: pl.BlockSpec((1, tm, w), lambda i, j: (i, j, 0))
    mod = pl.BlockSpec((1, 1, d), lambda i, j: (i, 0, 0))
    return pl.pallas_call(
        _outmlp_kernel,
        grid=(b, s // tm),
        in_specs=[row(d), row(GROUP), row(GROUP), row(GROUP), row(GROUP), mod, mod, mod, mod,
                  _const_spec((1, d)), _const_spec(wo.shape), _const_spec(w1.shape), _const_spec(w2.shape)],
        out_specs=row(d),
        out_shape=jax.ShapeDtypeStruct((b, s, d), F32),
        compiler_params=_cparams(("arbitrary", "arbitrary")),
        name="out_mlp",
    )(x, ya, yb, yc, yd, gate1, shift2, scale2, gate2, gn2, wo, w1, w2)


def _placement_matrices():
    src = jnp.arange(GROUP)
    dst = (src // HEAD_DIM) * AUG + src % HEAD_DIM
    pk = jnp.zeros((GROUP, N_HEADS * AUG), BF16).at[src, dst].set(1.0)
    head = jnp.arange(N_HEADS)[:, None, None]
    r = jnp.arange(GROUP)[None, :, None]
    c = jnp.arange(AUG)[None, None, :]
    po = ((r // HEAD_DIM == head) & (c == r % HEAD_DIM)).astype(BF16)
    po = jnp.transpose(po, (0, 2, 1))
    rs = jnp.arange(N_HEADS * GATE_WIDTH)
    dst_s = (rs // GATE_WIDTH) * AUG + GATE_LANE0 + rs % GATE_WIDTH
    pq = jnp.concatenate([pk, jnp.zeros((N_HEADS * GATE_WIDTH, N_HEADS * AUG), BF16).at[rs, dst_s].set(1.0)],
                         axis=0)
    eh = (src[:, None] // HEAD_DIM == src[None, :] // HEAD_DIM).astype(BF16)
    return pk, pq, po, eh


def _pick_tile(s, pref):
    t = pref
    while s % t:
        t //= 2
    return t


def kernel(x, c, positions, w_ada, b_ada, g_norm1, w_in, w_sconv, w_cconv, b_cconv, g_cln, b_cln, g_q, g_k,
           w_out, g_norm2, w_mlp1, w_mlp2):
    b, s, d = x.shape
    depth = w_ada.shape[0]
    nb = s // MOBA_BLOCK
    assert d == D_MODEL and s % MOBA_KV_TILE == 0 and nb <= GATE_WIDTH
    tm = _pick_tile(s, 512)
    pk, pq, po, eh = _placement_matrices()
    mod = _ada_call(c, w_ada, b_ada).reshape(depth, b, 6, 1, d)
    cos_t, sin_t = _rope_call(positions)
    tile4 = lambda v: jnp.tile(v, N_HEADS).reshape(1, GROUP)
    for l in range(depth):
        shift1, scale1, gate1, shift2, scale2, gate2 = (mod[l, :, i] for i in range(6))
        ya, yd, sbq, sbk, sbv, qaug, kaug, vaug = _inproj_call(
            x, shift1, scale1, g_norm1[l].reshape(1, d), w_in[l].astype(BF16), cos_t, sin_t,
            tile4(g_q[l]), tile4(g_k[l]), pk, eh, pq, w_sconv[l], w_cconv[l], b_cconv[l].reshape(1, GROUP),
            g_cln[l].reshape(1, GROUP), b_cln[l].reshape(1, GROUP), tm)
        yb = _sb_call(sbq, sbk, sbv)
        yc = _moba_call(qaug, kaug, vaug, po)
        x = _outmlp_call(x, ya, yb, yc, yd, gate1, shift2, scale2, gate2, g_norm2[l].reshape(1, d),
                         w_out[l].astype(BF16), w_mlp1[l].astype(BF16), w_mlp2[l].astype(BF16), tm)
    return x
```

```python
import functools
import math

import jax
import jax.numpy as jnp
from jax import lax
from jax.experimental import pallas as pl
from jax.experimental.pallas import tpu as pltpu

F32 = jnp.float32
BF16 = jnp.bfloat16
HIGHEST = lax.Precision.HIGHEST

D_MODEL = 1024
GROUP = 256
HEAD_DIM = 64
N_HEADS = GROUP // HEAD_DIM
IN_WIDTH = 11 * GROUP
D_FF = 4 * D_MODEL
SHORT_K = 3
CONF_K = 31
CONV_HALO = 32
SHORT_HALO = 8
CONV_CHUNK = 64
SB_BLOCK = 128
MOBA_BLOCK = 256
MOBA_TOPK = 3
MOBA_KV_TILE = 512
MOBA_Q_TILE = 512
ROPE_THETA = 10000.0
EPS = 1e-6
LANES = 128
SUBLANES = 8
AUG = 128
GATE_LANE0 = HEAD_DIM
GATE_WIDTH = 32
MASK_BIG = 2.0 ** 100
SB_EXIT = -90.0
VMEM_LIMIT = 56 * 1024 * 1024


def _log2(n):
    assert n & (n - 1) == 0
    return n.bit_length() - 1


def _cparams(sem):
    return pltpu.CompilerParams(dimension_semantics=sem, vmem_limit_bytes=VMEM_LIMIT)


def _split_bf16(a):
    hi = a.astype(BF16)
    return hi, (a - hi.astype(F32)).astype(BF16)


def _const_spec(shape):
    n = len(shape)
    return pl.BlockSpec(shape, lambda *_: (0,) * n, pipeline_mode=pl.Buffered(1))


def _ada_kernel(c_ref, w_ref, b_ref, o_ref):
    c = c_ref[...]
    ca = c * jax.nn.sigmoid(c)
    o_ref[0] = jnp.dot(ca, w_ref[0], precision=HIGHEST, preferred_element_type=F32) + b_ref[0]


def _ada_call(c, w_ada, b_ada):
    depth, d, n = w_ada.shape
    b = c.shape[0]
    tn = 1536
    return pl.pallas_call(
        _ada_kernel,
        grid=(depth, n // tn),
        in_specs=[pl.BlockSpec((b, d), lambda l, j: (0, 0)),
                  pl.BlockSpec((1, d, tn), lambda l, j: (l, 0, j)),
                  pl.BlockSpec((1, 1, tn), lambda l, j: (l, 0, j))],
        out_specs=pl.BlockSpec((1, b, tn), lambda l, j: (l, 0, j)),
        out_shape=jax.ShapeDtypeStruct((depth, b, n), F32),
        compiler_params=_cparams(("arbitrary", "arbitrary")),
        name="ada_mod",
    )(c, w_ada, b_ada.reshape(depth, 1, n))


def _rope_kernel(pos_ref, cos_ref, sin_ref):
    pos = pos_ref[0].astype(F32)
    lane = lax.broadcasted_iota(jnp.int32, (1, LANES), 1)
    half = HEAD_DIM // 2
    idx = (lane % half).astype(F32)
    inv_freq = jnp.exp((-math.log(ROPE_THETA) * idx) / half)
    ang = pos * inv_freq
    sign = jnp.where((lane % HEAD_DIM) < half, -1.0, 1.0)
    cos_ref[0] = jnp.cos(ang)
    sin_ref[0] = jnp.sin(ang) * sign


def _rope_call(positions):
    b, s = positions.shape
    ts = 1024
    spec = pl.BlockSpec((1, ts, LANES), lambda i, j: (i, j, 0))
    return pl.pallas_call(
        _rope_kernel,
        grid=(b, s // ts),
        in_specs=[pl.BlockSpec((1, ts, 1), lambda i, j: (i, j, 0))],
        out_specs=[spec, spec],
        out_shape=[jax.ShapeDtypeStruct((b, s, LANES), F32)] * 2,
        compiler_params=_cparams(("arbitrary", "arbitrary")),
        name="rope_table",
    )(positions.reshape(b, s, 1))


def _inproj_kernel(x_ref, sh_ref, sc_ref, g1_ref, w_ref, cos_ref, sin_ref, gq_ref, gk_ref, pk_ref,
                   eh_ref, pq_ref, ws_ref, wc_ref, bc_ref, gl_ref, bl_ref,
                   ya_ref, yd_ref, sbq_ref, sbk_ref, sbv_ref, qaug_ref, kaug_ref, vaug_ref,
                   km_scr, scb_scr, exts, extc, shc):
    tm = x_ref.shape[1]
    g = GROUP
    nbt = tm // MOBA_BLOCK
    jstep = pl.program_id(1)

    @pl.when(jstep == 0)
    def _():
        km_scr[...] = jnp.zeros(km_scr.shape, F32)
        exts[0:SHORT_HALO] = jnp.zeros((SHORT_HALO, g), F32)
        extc[0:CONV_HALO] = jnp.zeros((CONV_HALO, g), F32)

    @pl.when(jstep > 0)
    def _():
        exts[0:SHORT_HALO] = exts[tm:tm + SHORT_HALO]
        extc[0:CONV_HALO] = extc[tm:tm + CONV_HALO]

    x = x_ref[0]
    ms = jnp.mean(x * x, axis=-1, keepdims=True)
    h = x * lax.rsqrt(ms + EPS) * g1_ref[...]
    h = h * (1.0 + sc_ref[0]) + sh_ref[0]
    hb = h.astype(BF16)

    def proj(c0, n):
        return jnp.dot(hb, w_ref[:, c0:c0 + n], preferred_element_type=F32)

    p = proj(0, 3 * g)
    scb_scr[...] = p[:, :g]
    exts[SHORT_HALO:SHORT_HALO + tm] = p[:, g:2 * g] * p[:, 2 * g:]

    p = proj(9 * g, 2 * g)
    extc[CONV_HALO:CONV_HALO + tm] = p[:, :g] * jax.nn.sigmoid(p[:, g:])

    n_ext = tm + CONV_HALO
    for r in range(SUBLANES):
        n = n_ext - (SUBLANES if r else 0)
        shc[r, 0:n] = extc[r:r + n]

    def conv_chunks(first, stop):
        for c0 in range(first * CONV_CHUNK, stop * CONV_CHUNK, CONV_CHUNK):
            acc = jnp.zeros((CONV_CHUNK, g), F32)
            for kk in range(SHORT_K):
                o = SHORT_HALO - (SHORT_K - 1) + kk + c0
                acc = acc + ws_ref[kk:kk + 1, :] * exts[o:o + CONV_CHUNK]
            ya_ref[0, c0:c0 + CONV_CHUNK] = (scb_scr[c0:c0 + CONV_CHUNK] * acc).astype(BF16)

            acc = jnp.zeros((CONV_CHUNK, g), F32)
            for kk in range(CONF_K):
                o = CONV_HALO - (CONF_K - 1) + kk
                a0 = o - o % SUBLANES + c0
                acc = acc + wc_ref[kk:kk + 1, :] * shc[o % SUBLANES, a0:a0 + CONV_CHUNK]
            u = acc + bc_ref[...]
            mu = jnp.mean(u, axis=-1, keepdims=True)
            var = jnp.mean(jnp.square(u - mu), axis=-1, keepdims=True)
            y = (u - mu) * lax.rsqrt(var + EPS) * gl_ref[...] + bl_ref[...]
            yd_ref[0, c0:c0 + CONV_CHUNK] = (y * jax.nn.sigmoid(y)).astype(BF16)

    n_chunks = tm // CONV_CHUNK

    p = proj(3 * g, 3 * g)
    sbq_ref[0] = (p[:, :g] * (1.0 / math.sqrt(HEAD_DIM))).astype(BF16)
    sbk_ref[0] = p[:, g:2 * g].astype(BF16)
    sbv_ref[0] = p[:, 2 * g:].astype(BF16)
    conv_chunks(0, n_chunks // 4)

    p = proj(6 * g, 3 * g)
    cos_t = cos_ref[0]
    sin_t = sin_ref[0]
    lane = lax.broadcasted_iota(jnp.int32, (tm, LANES), 1)
    first_half = (lane % HEAD_DIM) < (HEAD_DIM // 2)
    half = HEAD_DIM // 2

    def norm_rope(t, gain):
        sq_hi, sq_lo = _split_bf16(t * t)
        ss = (jnp.dot(sq_hi, eh_ref[...], preferred_element_type=F32)
              + jnp.dot(sq_lo, eh_ref[...], preferred_element_type=F32))
        y = t * lax.rsqrt(ss * (1.0 / HEAD_DIM) + EPS) * gain
        out = []
        for c0 in range(0, g, LANES):
            yh = y[:, c0:c0 + LANES]
            partner = jnp.where(first_half, pltpu.roll(yh, LANES - half, 1), pltpu.roll(yh, half, 1))
            out.append(yh * cos_t + partner * sin_t)
        return jnp.concatenate(out, axis=1)

    q = norm_rope(p[:, :g], gq_ref[...])
    k = norm_rope(p[:, g:2 * g], gk_ref[...])
    conv_chunks(n_chunks // 4, n_chunks // 2)

    for r in range(nbt):
        km_scr[pl.ds(jstep * nbt + r, 1), :] = jnp.mean(k[r * MOBA_BLOCK:(r + 1) * MOBA_BLOCK], axis=0,
                                                        keepdims=True)

    nt = (((1,), (1,)), ((), ()))
    gr = lax.broadcasted_iota(jnp.int32, (N_HEADS * GATE_WIDTH, g), 0)
    gc = lax.broadcasted_iota(jnp.int32, (N_HEADS * GATE_WIDTH, g), 1)
    same_head = jnp.right_shift(gr, _log2(GATE_WIDTH)) == jnp.right_shift(gc, _log2(HEAD_DIM))
    km4 = jnp.where(same_head, jnp.concatenate([km_scr[...]] * N_HEADS, axis=0), 0.0)
    k_hi, k_lo = _split_bf16(km4)
    q_hi, q_lo = _split_bf16(q)
    gate_t = (lax.dot_general(k_hi, q_hi, nt, preferred_element_type=F32)
              + lax.dot_general(k_hi, q_lo, nt, preferred_element_type=F32)
              + lax.dot_general(k_lo, q_hi, nt, preferred_element_type=F32))
    blk_id = lax.broadcasted_iota(jnp.int32, (GATE_WIDTH, tm), 0)
    own = jstep * nbt + jnp.right_shift(lax.broadcasted_iota(jnp.int32, (GATE_WIDTH, tm), 1), _log2(MOBA_BLOCK))
    sel_rows = []
    for hh in range(N_HEADS):
        gsc = jnp.where(blk_id < own, gate_t[hh * GATE_WIDTH:(hh + 1) * GATE_WIDTH], -jnp.inf)
        sel = blk_id == own
        for _ in range(MOBA_TOPK):
            mx = jnp.max(gsc, axis=0, keepdims=True)
            first = jnp.min(jnp.where(gsc == mx, blk_id, GATE_WIDTH), axis=0, keepdims=True)
            pick = jnp.logical_and(blk_id == first, mx > -jnp.inf)
            sel = jnp.logical_or(sel, pick)
            gsc = jnp.where(pick, -jnp.inf, gsc)
        sel_rows.append(jnp.where(sel, 1.0, 0.0))
    sel_q = jnp.transpose(jnp.concatenate(sel_rows, axis=0))
    conv_chunks(n_chunks // 2, 3 * n_chunks // 4)

    lane_a = lax.broadcasted_iota(jnp.int32, (tm, N_HEADS * AUG), 1) % AUG
    in_gate = jnp.logical_and(lane_a >= GATE_LANE0, lane_a < GATE_LANE0 + GATE_WIDTH)
    qsel = jnp.concatenate([(q * (1.0 / math.sqrt(HEAD_DIM))).astype(BF16), sel_q.astype(BF16)], axis=1)
    qa = jnp.dot(qsel, pq_ref[...], preferred_element_type=F32)
    qaug_ref[0] = jnp.where(in_gate, jnp.where(qa > 0.5, 0.0, -MASK_BIG), qa).astype(BF16)

    row = lax.broadcasted_iota(jnp.int32, (tm, N_HEADS * AUG), 0)
    blk = jstep * nbt + jnp.right_shift(row, _log2(MOBA_BLOCK))
    kaug = jnp.dot(k.astype(BF16), pk_ref[...], preferred_element_type=F32)
    kaug_ref[0] = jnp.where(lane_a - GATE_LANE0 == blk, 1.0, kaug).astype(BF16)
    vaug = jnp.dot(p[:, 2 * g:].astype(BF16), pk_ref[...], preferred_element_type=F32)
    vaug_ref[0] = jnp.where(lane_a >= HEAD_DIM, 1.0, vaug).astype(BF16)

    conv_chunks(3 * n_chunks // 4, n_chunks)


def _inproj_call(x, shift, scale, g1, w_in, cos_t, sin_t, gq, gk, pk, eh, pq, ws, wc, bc, gl, bl, tm):
    b, s, d = x.shape
    g = GROUP
    wa = N_HEADS * AUG
    row = lambda w: pl.BlockSpec((1, tm, w), lambda i, j: (i, j, 0))
    mod = pl.BlockSpec((1, 1, d), lambda i, j: (i, 0, 0))
    bf16o = lambda w: jax.ShapeDtypeStruct((b, s, w), BF16)
    return pl.pallas_call(
        _inproj_kernel,
        grid=(b, s // tm),
        in_specs=[row(d), mod, mod, _const_spec((1, d)), _const_spec(w_in.shape), row(LANES), row(LANES),
                  _const_spec((1, g)), _const_spec((1, g)), _const_spec(pk.shape), _const_spec(eh.shape),
                  _const_spec(pq.shape), _const_spec(ws.shape), _const_spec(wc.shape), _const_spec((1, g)),
                  _const_spec((1, g)), _const_spec((1, g))],
        out_specs=[row(g), row(g), row(g), row(g), row(g),
                   row(wa), row(wa), row(wa)],
        out_shape=[bf16o(g), bf16o(g), bf16o(g), bf16o(g), bf16o(g), bf16o(wa), bf16o(wa), bf16o(wa)],
        scratch_shapes=[pltpu.VMEM((GATE_WIDTH, g), F32), pltpu.VMEM((tm, g), F32),
                        pltpu.VMEM((tm + SHORT_HALO, g), F32), pltpu.VMEM((tm + CONV_HALO, g), F32),
                        pltpu.VMEM((SUBLANES, tm + CONV_HALO, g), F32)],
        compiler_params=_cparams(("arbitrary", "arbitrary")),
        name="in_proj",
    )(x, shift, scale, g1, w_in, cos_t, sin_t, gq, gk, pk, eh, pq, ws, wc, bc, gl, bl)


def _softplus(z):
    return jnp.maximum(z, 0.0) + jnp.log(1.0 + jnp.exp(-jnp.abs(z)))


def _suffix_sum(l, upper):
    hi, lo = _split_bf16(l)
    return (jnp.dot(hi, upper, preferred_element_type=F32) + jnp.dot(lo, upper, preferred_element_type=F32))


def _sb_kernel(q_ref, k_ref, v_ref, o_ref, r_ref, acc_ref, z_ref, c_ref):
    t = SB_BLOCK
    w = SB_WINDOW * t
    n_sub = q_ref.shape[1] // t
    step = pl.program_id(1)
    ur = lax.broadcasted_iota(jnp.int32, (t, t), 0)
    uc = lax.broadcasted_iota(jnp.int32, (t, t), 1)
    upper = (ur >= uc).astype(BF16)
    low_head = lax.broadcasted_iota(jnp.int32, (t, t), 1) < HEAD_DIM
    nt = (((1,), (1,)), ((), ()))

    def qmask(u, h):
        pair = q_ref[0, u * t:(u + 1) * t, (h // 2) * LANES:(h // 2 + 1) * LANES]
        keep = low_head if h % 2 == 0 else jnp.logical_not(low_head)
        return jnp.where(keep, pair, jnp.zeros_like(pair))

    def kv(ref, start, n, h):
        return ref[0, pl.ds(start, n), (h // 2) * LANES:(h // 2 + 1) * LANES]

    chains = [(u, h) for u in range(n_sub) for h in range(N_HEADS)]
    first_tile, wstarts, valids = [], [], []
    for u in range(n_sub):
        i = step * n_sub + u
        first = jnp.maximum(i - (SB_WINDOW - 1), 0)
        wstart = pl.multiple_of(first * t, t)
        kpos = wstart + lax.broadcasted_iota(jnp.int32, (t, w), 1)
        qpos = i * t + lax.broadcasted_iota(jnp.int32, (t, w), 0)
        first_tile.append(first)
        wstarts.append(wstart)
        valids.append(kpos < qpos)
    for u, h in chains:
        z_ref[u, h] = lax.dot_general(qmask(u, h), kv(k_ref, wstarts[u], w, h), nt, preferred_element_type=F32)
    for u, h in chains:
        l = jnp.where(valids[u], -_softplus(z_ref[u, h]), 0.0)
        later = None
        for b in reversed(range(SB_WINDOW)):
            cl = _suffix_sum(l[:, b * t:(b + 1) * t], upper)
            c_ref[u, h, :, b * t:(b + 1) * t] = cl if later is None else cl + later
            tot = jnp.broadcast_to(cl[:, 0:1], (t, t))
            later = tot if later is None else later + tot
    rmax0 = [jnp.float32(-jnp.inf)] * n_sub
    for u, h in chains:
        c = c_ref[u, h]
        a = jnp.where(valids[u], jnp.exp(z_ref[u, h] + c), 0.0)
        acc_ref[u, h] = jnp.dot(a.astype(BF16), kv(v_ref, wstarts[u], w, h), preferred_element_type=F32)
        r = jnp.broadcast_to(c[:, 0:1], (t, t))
        r_ref[u, h] = r
        rmax0[u] = jnp.maximum(rmax0[u], jnp.max(r))

    def cond(carry):
        j, rmax = carry
        return jnp.logical_and(j >= 0, rmax > SB_EXIT)

    for u in range(n_sub):
        def body(carry, u=u):
            j, _ = carry
            start = pl.multiple_of(j * t, t)
            rmax = jnp.float32(-jnp.inf)
            for h in range(N_HEADS):
                z_ref[u, h, :, 0:t] = lax.dot_general(qmask(u, h), kv(k_ref, start, t, h), nt,
                                                      preferred_element_type=F32)
            for h in range(N_HEADS):
                c_ref[u, h, :, 0:t] = _suffix_sum(-_softplus(z_ref[u, h, :, 0:t]), upper)
            for h in range(N_HEADS):
                r = r_ref[u, h]
                c = c_ref[u, h, :, 0:t]
                a = jnp.exp(z_ref[u, h, :, 0:t] + c + r)
                acc_ref[u, h] = acc_ref[u, h] + jnp.dot(a.astype(BF16), kv(v_ref, start, t, h),
                                                        preferred_element_type=F32)
                r_new = r + jnp.broadcast_to(c[:, 0:1], (t, t))
                r_ref[u, h] = r_new
                rmax = jnp.maximum(rmax, jnp.max(r_new))
            return j - 1, rmax

        lax.while_loop(cond, body, (first_tile[u] - 1, rmax0[u]))

    for u in range(n_sub):
        for p in range(N_HEADS // 2):
            o_ref[0, u * t:(u + 1) * t, p * LANES:(p + 1) * LANES] = jnp.where(
                low_head, acc_ref[u, 2 * p], acc_ref[u, 2 * p + 1]).astype(BF16)


SB_STEP = 256
SB_WINDOW = 3


def _sb_call(q, k, v):
    b, s, g = q.shape
    t = SB_BLOCK
    n_sub = SB_STEP // t
    seq = pl.BlockSpec((1, s, g), lambda i, j: (i, 0, 0))
    tile = pl.BlockSpec((1, SB_STEP, g), lambda i, j: (i, j, 0))
    return pl.pallas_call(
        _sb_kernel,
        grid=(b, s // SB_STEP),
        in_specs=[tile, seq, seq],
        out_specs=tile,
        out_shape=jax.ShapeDtypeStruct((b, s, g), BF16),
        scratch_shapes=[pltpu.VMEM((n_sub, N_HEADS, t, t), F32), pltpu.VMEM((n_sub, N_HEADS, t, t), F32),
                        pltpu.VMEM((n_sub, N_HEADS, t, SB_WINDOW * t), F32),
                        pltpu.VMEM((n_sub, N_HEADS, t, SB_WINDOW * t), F32)],
        compiler_params=_cparams(("arbitrary", "arbitrary")),
        name="stick_breaking",
    )(q, k, v)


def _moba_kernel(qa_ref, kaug_ref, vaug_ref, po_ref, o_ref, m_ref, acc_ref, sa_ref, sb_ref):
    t = MOBA_Q_TILE
    tk = MOBA_KV_TILE
    i = pl.program_id(1)
    lane = lax.broadcasted_iota(jnp.int32, (t, AUG), 1)
    nt = (((1,), (1,)), ((), ()))

    def kblk(ref, start, n, h):
        return ref[0, pl.ds(start, n), h * AUG:(h + 1) * AUG]

    for h in range(N_HEADS):
        m_ref[h] = jnp.full((t, AUG), -jnp.inf, F32)
        acc_ref[h] = jnp.zeros((t, AUG), F32)

    def scores(j, dst_ref):
        start = pl.multiple_of(j * tk, tk)
        for h in range(N_HEADS):
            dst_ref[h] = lax.dot_general(qa_ref[0, :, h * AUG:(h + 1) * AUG],
                                         kblk(kaug_ref, start, tk, h), nt, preferred_element_type=F32)

    def softmax_pv(j, src_ref, mask=None):
        start = pl.multiple_of(j * tk, tk)
        for h in range(N_HEADS):
            s = src_ref[h]
            if mask is not None:
                s = jnp.where(mask, s, -jnp.inf)
            m_old = m_ref[h]
            m_new = jnp.maximum(m_old, jnp.max(s, axis=-1, keepdims=True))
            alpha = jnp.exp(m_old - m_new)
            p = jnp.exp(s - jnp.concatenate([m_new] * (tk // AUG), axis=1))
            acc_ref[h] = acc_ref[h] * alpha + jnp.dot(p.astype(BF16), kblk(vaug_ref, start, tk, h),
                                                      preferred_element_type=F32)
            m_ref[h] = m_new

    n_past = jnp.right_shift(i * t, _log2(tk))
    last = jnp.maximum(n_past - 1, 0)
    drow = lax.broadcasted_iota(jnp.int32, (t, tk), 0)
    dcol = lax.broadcasted_iota(jnp.int32, (t, tk), 1)
    diag_mask = n_past * tk + dcol <= i * t + drow
    scores(n_past, sa_ref)
    scores(0, sb_ref)
    softmax_pv(n_past, sa_ref, diag_mask)

    def body(jj, carry):
        a = 2 * jj
        scores(a + 1, sa_ref)
        softmax_pv(a, sb_ref)
        scores(jnp.minimum(a + 2, last), sb_ref)
        softmax_pv(a + 1, sa_ref)
        return carry

    lax.fori_loop(0, jnp.right_shift(n_past, 1), body, 0)

    @pl.when(jnp.bitwise_and(n_past, 1) == 1)
    def _():
        softmax_pv(last, sb_ref)

    out = jnp.zeros((t, GROUP), F32)
    for h in range(N_HEADS):
        acc = acc_ref[h]
        denom = pltpu.roll(acc, HEAD_DIM, 1)
        y = jnp.where(lane < HEAD_DIM, acc / denom, 0.0).astype(BF16)
        out = out + jnp.dot(y, po_ref[h], preferred_element_type=F32)
    o_ref[0] = out.astype(BF16)


def _moba_call(qaug, kaug, vaug, po):
    b, s, wa = qaug.shape
    g = GROUP
    t = MOBA_Q_TILE
    assert MOBA_KV_TILE % t == 0 and s % MOBA_KV_TILE == 0
    seq = pl.BlockSpec((1, s, wa), lambda i, j: (i, 0, 0))
    tile = pl.BlockSpec((1, t, g), lambda i, j: (i, j, 0))
    return pl.pallas_call(
        _moba_kernel,
        grid=(b, s // t),
        in_specs=[pl.BlockSpec((1, t, wa), lambda i, j: (i, j, 0)), seq, seq, _const_spec(po.shape)],
        out_specs=tile,
        out_shape=jax.ShapeDtypeStruct((b, s, g), BF16),
        scratch_shapes=[pltpu.VMEM((N_HEADS, t, AUG), F32), pltpu.VMEM((N_HEADS, t, AUG), F32),
                        pltpu.VMEM((N_HEADS, t, MOBA_KV_TILE), F32), pltpu.VMEM((N_HEADS, t, MOBA_KV_TILE), F32)],
        compiler_params=_cparams(("arbitrary", "arbitrary")),
        name="moba",
    )(qaug, kaug, vaug, po)


MLP_CHUNK = 512


def _outmlp_kernel(x_ref, ya_ref, yb_ref, yc_ref, yd_ref, g1_ref, sh_ref, sc_ref, g2_ref, gn_ref, wo_ref,
                   w1_ref, w2_ref, o_ref):
    g = GROUP
    mix = jnp.dot(ya_ref[0], wo_ref[0:g], preferred_element_type=F32)
    mix = mix + jnp.dot(yb_ref[0], wo_ref[g:2 * g], preferred_element_type=F32)
    mix = mix + jnp.dot(yc_ref[0], wo_ref[2 * g:3 * g], preferred_element_type=F32)
    mix = mix + jnp.dot(yd_ref[0], wo_ref[3 * g:4 * g], preferred_element_type=F32)
    x1 = x_ref[0] + g1_ref[0] * mix
    ms = jnp.mean(x1 * x1, axis=-1, keepdims=True)
    h = x1 * lax.rsqrt(ms + EPS) * gn_ref[...]
    hb = (h * (1.0 + sc_ref[0]) + sh_ref[0]).astype(BF16)
    acc = jnp.zeros(x1.shape, F32)
    for f0 in range(0, D_FF, MLP_CHUNK):
        a = jnp.maximum(jnp.dot(hb, w1_ref[:, f0:f0 + MLP_CHUNK], preferred_element_type=F32), 0.0)
        acc = acc + jnp.dot((a * a).astype(BF16), w2_ref[f0:f0 + MLP_CHUNK, :], preferred_element_type=F32)
    o_ref[0] = x1 + g2_ref[0] * acc


def _outmlp_call(x, ya, yb, yc, yd, gate1, shift2, scale2, gate2, gn2, wo, w1, w2, tm):
    b, s, d = x.shape
    row = lambda w: pl.BlockSpec((1, tm, w), lambda i, j: (i, j, 0))
    mod = pl.BlockSpec((1, 1, d), lambda i, j: (i, 0, 0))
    return pl.pallas_call(
        _outmlp_kernel,
        grid=(b, s // tm),
        in_specs=[row(d), row(GROUP), row(GROUP), row(GROUP), row(GROUP), mod, mod, mod, mod,
                  _const_spec((1, d)), _const_spec(wo.shape), _const_spec(w1.shape), _const_spec(w2.shape)],
        out_specs=row(d),
        out_shape=jax.ShapeDtypeStruct((b, s, d), F32),
        compiler_params=_cparams(("arbitrary", "arbitrary")),
        name="out_mlp",
    )(x, ya, yb, yc, yd, gate1, shift2, scale2, gate2, gn2, wo, w1, w2)


def _placement_matrices():
    src = jnp.arange(GROUP)
    dst = (src // HEAD_DIM) * AUG + src % HEAD_DIM
    pk = jnp.zeros((GROUP, N_HEADS * AUG), BF16).at[src, dst].set(1.0)
    head = jnp.arange(N_HEADS)[:, None, None]
    r = jnp.arange(GROUP)[None, :, None]
    c = jnp.arange(AUG)[None, None, :]
    po = ((r // HEAD_DIM == head) & (c == r % HEAD_DIM)).astype(BF16)
    po = jnp.transpose(po, (0, 2, 1))
    rs = jnp.arange(N_HEADS * GATE_WIDTH)
    dst_s = (rs // GATE_WIDTH) * AUG + GATE_LANE0 + rs % GATE_WIDTH
    pq = jnp.concatenate([pk, jnp.zeros((N_HEADS * GATE_WIDTH, N_HEADS * AUG), BF16).at[rs, dst_s].set(1.0)],
                         axis=0)
    eh = (src[:, None] // HEAD_DIM == src[None, :] // HEAD_DIM).astype(BF16)
    return pk, pq, po, eh


def _pick_tile(s, pref):
    t = pref
    while s % t:
        t //= 2
    return t


def kernel(x, c, positions, w_ada, b_ada, g_norm1, w_in, w_sconv, w_cconv, b_cconv, g_cln, b_cln, g_q, g_k,
           w_out, g_norm2, w_mlp1, w_mlp2):
    b, s, d = x.shape
    depth = w_ada.shape[0]
    nb = s // MOBA_BLOCK
    assert d == D_MODEL and s % MOBA_KV_TILE == 0 and nb <= GATE_WIDTH
    tm = _pick_tile(s, 512)
    pk, pq, po, eh = _placement_matrices()
    mod = _ada_call(c, w_ada, b_ada).reshape(depth, b, 6, 1, d)
    cos_t, sin_t = _rope_call(positions)
    tile4 = lambda v: jnp.tile(v, N_HEADS).reshape(1, GROUP)
    for l in range(depth):
        shift1, scale1, gate1, shift2, scale2, gate2 = (mod[l, :, i] for i in range(6))
        ya, yd, sbq, sbk, sbv, qaug, kaug, vaug = _inproj_call(
            x, shift1, scale1, g_norm1[l].reshape(1, d), w_in[l].astype(BF16), cos_t, sin_t,
            tile4(g_q[l]), tile4(g_k[l]), pk, eh, pq, w_sconv[l], w_cconv[l], b_cconv[l].reshape(1, GROUP),
            g_cln[l].reshape(1, GROUP), b_cln[l].reshape(1, GROUP), tm)
        yb = _sb_call(sbq, sbk, sbv)
        yc = _moba_call(qaug, kaug, vaug, po)
        x = _outmlp_call(x, ya, yb, yc, yd, gate1, shift2, scale2, gate2, g_norm2[l].reshape(1, d),
                         w_out[l].astype(BF16), w_mlp1[l].astype(BF16), w_mlp2[l].astype(BF16), tm)
    return x
```
